```python
import jax, jax.numpy as jnp
from jax import lax
import numpy as np

D_MODEL = 1024
BATCH = 2
SEQ = 16384
DEPTH = 2

N_MIXERS = 2
D_FF = 2816
FFN_RES = 0.5
CONV_WIDTH = 31
CONV_PAD = (CONV_WIDTH - 1) // 2
GLU_WIDTH = 2 * D_MODEL
HGRN_HEAD_DIM = 128
HGRN_HEADS = D_MODEL // HGRN_HEAD_DIM
HGRN_N_PROJ = 5
CHUNK = 64
EPS = 1e-6
N_CONV_LAYERS = (DEPTH + 1) // 2
N_HGRN_LAYERS = DEPTH // 2

kernel_name = "conformer_hgrn2_interleaved_encoder"


def rms_norm(x, g):
    xf = x.astype(jnp.float32)
    y = xf * lax.rsqrt(jnp.mean(jnp.square(xf), axis=-1, keepdims=True) + EPS)
    return (y * g).astype(x.dtype)


def layer_norm(x, g, b):
    xf = x.astype(jnp.float32)
    mu = jnp.mean(xf, axis=-1, keepdims=True)
    var = jnp.mean(jnp.square(xf - mu), axis=-1, keepdims=True)
    return ((xf - mu) * lax.rsqrt(var + EPS) * g + b).astype(x.dtype)


def swiglu(h, w13, w2):
    gate, up = jnp.split(h @ w13, 2, axis=-1)
    return (jax.nn.silu(gate) * up) @ w2


def conformer_conv(h, w_pw1, b_pw1, w_dw, b_dw, ln_g, ln_b, w_pw2, b_pw2):
    a, gate = jnp.split(h @ w_pw1 + b_pw1, 2, axis=-1)
    u = a * jax.nn.sigmoid(gate)
    u = lax.conv_general_dilated(
        u, w_dw[:, None, :].astype(u.dtype), window_strides=(1,),
        padding=[(CONV_PAD, CONV_PAD)],
        dimension_numbers=("NWC", "WIO", "NWC"),
        feature_group_count=D_MODEL) + b_dw
    u = layer_norm(u, ln_g, ln_b)
    return jax.nn.silu(u) @ w_pw2 + b_pw2


def chunk_gated_linear_recurrence(q, k, v, logf):
    B, S, H, DK = q.shape
    DV = v.shape[-1]
    nc = S // CHUNK

    def to_chunks(t):
        return t.reshape(B, nc, CHUNK, H, t.shape[-1]).transpose(1, 0, 3, 2, 4)

    causal_in_chunk = jnp.tril(jnp.ones((CHUNK, CHUNK), dtype=bool))[:, :, None]

    def step(state, xs):
        qc, kc, vc, lf = xs
        b = jnp.cumsum(lf, axis=2)
        diff = b[:, :, :, None, :] - b[:, :, None, :, :]
        decay = jnp.exp(jnp.where(causal_in_chunk, diff, -jnp.inf))
        scores = jnp.einsum("bhtk,bhsk,bhtsk->bhts", qc, kc, decay)
        o = (jnp.einsum("bhts,bhsv->bhtv", scores, vc)
             + jnp.einsum("bhtk,bhkv->bhtv", qc * jnp.exp(b), state))
        b_last = b[:, :, -1:, :]
        state = (jnp.exp(b_last)[:, :, 0, :, None] * state
                 + jnp.einsum("bhsk,bhsv->bhkv", kc * jnp.exp(b_last - b), vc))
        return state, o

    s0 = jnp.zeros((B, H, DK, DV), jnp.float32)
    _, outs = lax.scan(step, s0, (to_chunks(q), to_chunks(k), to_chunks(v), to_chunks(logf)))
    return outs.transpose(1, 0, 3, 2, 4).reshape(B, S, H, DV)


def hgrn2_bidirectional(h, w_in, lb_fwd, lb_bwd, gn_g, w_out):
    B, S, _ = h.shape
    H, DK = HGRN_HEADS, HGRN_HEAD_DIM
    proj = h @ w_in
    q, i, zf_f, zf_b, g = jnp.split(proj, HGRN_N_PROJ, axis=-1)
    q = jax.nn.silu(q.astype(jnp.float32)).reshape(B, S, H, DK)
    v = i.astype(jnp.float32).reshape(B, S, H, DK)

    def gates(zf, lb):
        z = zf.astype(jnp.float32).reshape(B, S, H, DK)
        lbh = lb.reshape(H, DK)
        logf = jnp.log(lbh + (1.0 - lbh) * jax.nn.sigmoid(z))
        k = (1.0 - lbh) * jax.nn.sigmoid(-z)
        return logf, k

    logf_f, k_f = gates(zf_f, lb_fwd)
    logf_b, k_b = gates(zf_b, lb_bwd)
    o_fwd = chunk_gated_linear_recurrence(q, k_f, v, logf_f)
    o_bwd = jnp.flip(chunk_gated_linear_recurrence(
        jnp.flip(q, 1), jnp.flip(k_b, 1), jnp.flip(v, 1), jnp.flip(logf_b, 1)), 1)
    o = o_fwd + o_bwd
    o = o * lax.rsqrt(jnp.mean(jnp.square(o), axis=-1, keepdims=True) + EPS)
    o = o * gn_g.reshape(H, DK) * jax.nn.silu(g.astype(jnp.float32)).reshape(B, S, H, DK)
    return o.reshape(B, S, D_MODEL).astype(h.dtype) @ w_out


def setup_inputs(seed: int = 0) -> dict:
    key = jax.random.key(seed)
    ks = jax.random.split(key, 20)
    D, F = D_MODEL, D_FF
    nrm = jax.random.normal
    f32 = jnp.float32
    return {
        "x": nrm(ks[0], (BATCH, SEQ, D), f32),
        "norm_g": 1.0 + 0.02 * nrm(ks[1], (DEPTH, 3, D), f32),
        "ffn_w13": nrm(ks[2], (DEPTH, 2, D, 2 * F), f32) * D ** -0.5,
        "ffn_w2": nrm(ks[3], (DEPTH, 2, F, D), f32) * F ** -0.5,
        "conv_w_pw1": nrm(ks[4], (N_CONV_LAYERS, D, GLU_WIDTH), f32) * D ** -0.5,
        "conv_b_pw1": 0.02 * nrm(ks[5], (N_CONV_LAYERS, GLU_WIDTH), f32),
        "conv_w_dw": nrm(ks[6], (N_CONV_LAYERS, CONV_WIDTH, D), f32) * CONV_WIDTH ** -0.5,
        "conv_b_dw": 0.02 * nrm(ks[7], (N_CONV_LAYERS, D), f32),
        "conv_ln_g": 1.0 + 0.02 * nrm(ks[8], (N_CONV_LAYERS, D), f32),
        "conv_ln_b": 0.02 * nrm(ks[9], (N_CONV_LAYERS, D), f32),
        "conv_w_pw2": nrm(ks[10], (N_CONV_LAYERS, D, D), f32) * D ** -0.5,
        "conv_b_pw2": 0.02 * nrm(ks[11], (N_CONV_LAYERS, D), f32),
        "hgrn_w_in": nrm(ks[12], (N_HGRN_LAYERS, D, HGRN_N_PROJ * D), f32) * D ** -0.5,
        "hgrn_lb": 0.1 * nrm(ks[13], (2, DEPTH, D), f32),
        "hgrn_gn_g": 1.0 + 0.02 * nrm(ks[14], (N_HGRN_LAYERS, D), f32),
        "hgrn_w_out": nrm(ks[15], (N_HGRN_LAYERS, D, D), f32) * D ** -0.5,
        "final_g": 1.0 + 0.02 * nrm(ks[16], (D,), f32),
    }


def reference(x, norm_g, ffn_w13, ffn_w2, conv_w_pw1, conv_b_pw1, conv_w_dw, conv_b_dw,
              conv_ln_g, conv_ln_b, conv_w_pw2, conv_b_pw2, hgrn_w_in, hgrn_lb,
              hgrn_gn_g, hgrn_w_out, final_g):
    lb_table = jnp.cumsum(jax.nn.softmax(hgrn_lb.astype(jnp.float32), axis=1), axis=1)
    lb_table = lb_table - lb_table[:, :1]
    for layer in range(DEPTH):
        x = x + FFN_RES * swiglu(rms_norm(x, norm_g[layer, 0]), ffn_w13[layer, 0], ffn_w2[layer, 0])
        h = rms_norm(x, norm_g[layer, 1])
        j = layer // N_MIXERS
        if layer % N_MIXERS == 0:
            mix = conformer_conv(h, conv_w_pw1[j], conv_b_pw1[j], conv_w_dw[j], conv_b_dw[j],
                                 conv_ln_g[j], conv_ln_b[j], conv_w_pw2[j], conv_b_pw2[j])
        else:
            mix = hgrn2_bidirectional(h, hgrn_w_in[j], lb_table[0, layer], lb_table[1, layer],
                                      hgrn_gn_g[j], hgrn_w_out[j])
        x = x + mix
        x = x + FFN_RES * swiglu(rms_norm(x, norm_g[layer, 2]), ffn_w13[layer, 1], ffn_w2[layer, 1])
    return rms_norm(x, final_g)
```

```python
import functools

import jax
import jax.numpy as jnp
from jax import lax
from jax.experimental import pallas as pl
from jax.experimental.pallas import tpu as pltpu

EPS = 1e-6
FFN_RES = 0.5
CONV_WIDTH = 31
CONV_PAD = (CONV_WIDTH - 1) // 2
HEAD_DIM = 128
N_PROJ = 5

SUBLANES = 8
LANES = 128
MXU_COLS = 256
HALO = 2 * SUBLANES
VMEM_LIMIT = 56 * 1024 * 1024

TOKEN_TILE = 512
SCAN_BLOCK = 256
SCAN_CHUNK = 64
CONV_STRIP = 32
CONV_LANES = 512
MAX_CHUNK_DECAY = 80.0

f32 = jnp.float32
bf16 = jnp.bfloat16


def _rms(x, g):
    return x * lax.rsqrt(jnp.mean(x * x, axis=-1, keepdims=True) + EPS) * g


def _resident(shape):
    return pl.BlockSpec(shape, lambda *_: (0,) * len(shape), pipeline_mode=pl.Buffered(1))


def _params(n_axes):
    return pltpu.CompilerParams(dimension_semantics=("arbitrary",) * n_axes,
                                vmem_limit_bytes=VMEM_LIMIT)


def _ffn_kernel(x_ref, g_ref, w13_ref, w2_ref, fg_ref, o_ref, a_ref, *, d_ff, fc, final):
    x = x_ref[...]
    h = _rms(x, g_ref[...]).astype(bf16)
    for j in range(d_ff // fc):
        gate = jnp.dot(h, w13_ref[:, j * fc:(j + 1) * fc], preferred_element_type=f32)
        up = jnp.dot(h, w13_ref[:, d_ff + j * fc:d_ff + (j + 1) * fc],
                     preferred_element_type=f32)
        a_ref[:, j * fc:(j + 1) * fc] = (jax.nn.silu(gate) * up).astype(bf16)
    y = jnp.dot(a_ref[...], w2_ref[...], preferred_element_type=f32)
    out = x + FFN_RES * y
    if final:
        out = _rms(out, fg_ref[...])
    o_ref[...] = out


def _ffn(x, g, w13, w2, final_g, *, final):
    t, d = x.shape
    d_ff = w2.shape[0]
    tm = min(TOKEN_TILE, t)
    fc = MXU_COLS if d_ff % MXU_COLS == 0 else d_ff
    row = pl.BlockSpec((tm, d), lambda i: (i, 0))
    return pl.pallas_call(
        functools.partial(_ffn_kernel, d_ff=d_ff, fc=fc, final=final),
        out_shape=jax.ShapeDtypeStruct((t, d), f32),
        grid=(t // tm,),
        in_specs=[row, _resident((1, d)), _resident(w13.shape), _resident(w2.shape),
                  _resident((1, d))],
        out_specs=row,
        scratch_shapes=[pltpu.VMEM((tm, d_ff), bf16)],
        compiler_params=_params(1),
        name="ffn",
    )(x, g.reshape(1, d), w13, w2, final_g.reshape(1, d))


def _conv_kernel(x_ref, xp_ref, xn_ref, g_ref, w1_ref, b1_ref, wdw_ref, bdw_ref,
                 lng_ref, lnb_ref, w2_ref, b2_ref, o_ref, u_ref, us_ref, c_ref, *, tm, d):
    i = pl.program_id(1)
    n = pl.num_programs(1)
    ext = tm + 2 * HALO
    xe = jnp.concatenate([xp_ref[...], x_ref[...], xn_ref[...]], axis=0)
    h = _rms(xe, g_ref[...]).astype(bf16)
    y = jnp.dot(h, w1_ref[...], preferred_element_type=f32) + b1_ref[...]
    u = y[:, :d] * jax.nn.sigmoid(y[:, d:])
    row = lax.broadcasted_iota(jnp.int32, (ext, 1), 0)
    inside = ((row >= HALO) | (i > 0)) & ((row < HALO + tm) | (i < n - 1))
    u_ref[...] = jnp.where(inside, u, 0.0)
    span = ext - SUBLANES
    for r in range(SUBLANES):
        us_ref[r] = u_ref[r:r + span, :]

    n_lane = d // CONV_LANES

    def strip(s, carry):
        r0 = pl.multiple_of(s * CONV_STRIP, CONV_STRIP)
        for c in range(n_lane):
            cs = slice(c * CONV_LANES, (c + 1) * CONV_LANES)
            acc = jnp.zeros((CONV_STRIP, CONV_LANES), f32)
            for k in range(CONV_WIDTH):
                off = k + HALO - CONV_PAD
                a, r = divmod(off, SUBLANES)
                tap = us_ref[r, pl.ds(r0 + a * SUBLANES, CONV_STRIP), cs]
                acc = acc + tap * wdw_ref[k:k + 1, cs]
            c_ref[pl.ds(r0, CONV_STRIP), cs] = acc
        return carry

    lax.fori_loop(0, tm // CONV_STRIP, strip, 0)

    cv = c_ref[...] + bdw_ref[...]
    mu = jnp.mean(cv, axis=-1, keepdims=True)
    cc = cv - mu
    var = jnp.mean(cc * cc, axis=-1, keepdims=True)
    z = cc * lax.rsqrt(var + EPS) * lng_ref[...] + lnb_ref[...]
    mix = jnp.dot(jax.nn.silu(z).astype(bf16), w2_ref[...],
                  preferred_element_type=f32) + b2_ref[...]
    o_ref[...] = x_ref[...] + mix


def _conv_mixer(x, g, w1, b1, wdw, bdw, lng, lnb, w2, b2):
    b, s, d = x.shape
    tm = min(TOKEN_TILE, s)
    nh = tm // HALO
    last = s // HALO - 1
    ext = tm + 2 * HALO
    main = pl.BlockSpec((None, tm, d), lambda bi, i: (bi, i, 0))
    prev = pl.BlockSpec((None, HALO, d), lambda bi, i: (bi, jnp.maximum(i * nh - 1, 0), 0))
    nxt = pl.BlockSpec((None, HALO, d), lambda bi, i: (bi, jnp.minimum((i + 1) * nh, last), 0))
    vec = lambda a: a.reshape(1, -1)
    return pl.pallas_call(
        functools.partial(_conv_kernel, tm=tm, d=d),
        out_shape=jax.ShapeDtypeStruct((b, s, d), f32),
        grid=(b, s // tm),
        in_specs=[main, prev, nxt, _resident((1, d)), _resident(w1.shape),
                  _resident((1, 2 * d)), _resident(wdw.shape), _resident((1, d)),
                  _resident((1, d)), _resident((1, d)), _resident(w2.shape),
                  _resident((1, d))],
        out_specs=main,
        scratch_shapes=[pltpu.VMEM((ext, d), f32),
                        pltpu.VMEM((SUBLANES, ext - SUBLANES, d), f32),
                        pltpu.VMEM((tm, d), f32)],
        compiler_params=_params(2),
        name="conv_mixer",
    )(x, x, x, vec(g), w1, vec(b1), wdw, vec(bdw), vec(lng), vec(lnb), w2, vec(b2))


def _hproj_kernel(x_ref, g_ref, w_ref, lbf_ref, lbb_ref,
                  q_ref, v_ref, kf_ref, lff_ref, kb_ref, lfb_ref, sg_ref, *, d):
    h = _rms(x_ref[...], g_ref[...]).astype(bf16)

    def proj(j):
        return jnp.dot(h, w_ref[:, j * d:(j + 1) * d], preferred_element_type=f32)

    q_ref[...] = jax.nn.silu(proj(0))
    v_ref[...] = proj(1).astype(bf16)
    for j, lb_ref, k_ref, lf_ref in ((2, lbf_ref, kf_ref, lff_ref), (3, lbb_ref, kb_ref, lfb_ref)):
        z = proj(j)
        lb = lb_ref[...]
        lf_ref[...] = jnp.log(lb + (1.0 - lb) * jax.nn.sigmoid(z))
        k_ref[...] = (1.0 - lb) * jax.nn.sigmoid(-z)
    sg_ref[...] = jax.nn.silu(proj(4))


def _hgrn_proj(x, g, w_in, lb_f, lb_b):
    t, d = x.shape
    tm = min(TOKEN_TILE, t)
    row = pl.BlockSpec((tm, d), lambda i: (i, 0))
    sd = lambda dt: jax.ShapeDtypeStruct((t, d), dt)
    return pl.pallas_call(
        functools.partial(_hproj_kernel, d=d),
        out_shape=(sd(f32), sd(bf16), sd(f32), sd(f32), sd(f32), sd(f32), sd(f32)),
        grid=(t // tm,),
        in_specs=[row, _resident((1, d)), _resident(w_in.shape), _resident((1, d)),
                  _resident((1, d))],
        out_specs=(row,) * 7,
        compiler_params=_params(1),
        name="hgrn_proj",
    )(x, g.reshape(1, d), w_in, lb_f.reshape(1, d), lb_b.reshape(1, d))


def _chunk_cumsum(lf, chunk, reverse):
    n = lf.shape[0]
    pos = lax.broadcasted_iota(jnp.int32, (n, 1), 0) % chunk
    b = lf
    shift = 1
    while shift < chunk:
        if reverse:
            moved = pltpu.roll(b, n - shift, axis=0)
            keep = pos < chunk - shift
        else:
            moved = pltpu.roll(b, shift, axis=0)
            keep = pos >= shift
        b = b + jnp.where(keep, moved, 0.0)
        shift *= 2
    return b


def _scan_factorised(q_ref, v_ref, k_ref, b_ref, o_ref, st_ref, *, reverse, chunk, heads):
    n = q_ref.shape[0]
    ti = lax.broadcasted_iota(jnp.int32, (chunk, chunk), 0)
    si = lax.broadcasted_iota(jnp.int32, (chunk, chunk), 1)
    causal = (si >= ti) if reverse else (si <= ti)
    order = range(n // chunk - 1, -1, -1) if reverse else range(n // chunk)
    for c in order:
        rs = slice(c * chunk, (c + 1) * chunk)
        b = b_ref[rs, :]
        b_end = b[:1] if reverse else b[chunk - 1:]
        mid = 0.5 * b_end
        e_mid = jnp.exp(mid)
        e_end = jnp.exp(b_end)
        qd = q_ref[rs, :] * jnp.exp(b - mid)
        kd = k_ref[rs, :] * jnp.exp(mid - b)
        qs = (qd * e_mid).astype(bf16)
        ks = (kd * e_mid).astype(bf16)
        qd = qd.astype(bf16)
        kd = kd.astype(bf16)
        for h in range(heads):
            hs = slice(h * HEAD_DIM, (h + 1) * HEAD_DIM)
            v = v_ref[rs, hs]
            sc = lax.dot_general(qd[:, hs], kd[:, hs], (((1,), (1,)), ((), ())),
                                 preferred_element_type=f32)
            sc = jnp.where(causal, sc, 0.0).astype(bf16)
            st = st_ref[h]
            o_ref[rs, hs] = (jnp.dot(sc, v, preferred_element_type=f32)
                             + lax.dot_general(qs[:, hs], st.astype(bf16),
                                               (((1,), (1,)), ((), ())),
                                               preferred_element_type=f32))
            st_ref[h] = st * e_end[:, hs] + lax.dot_general(
                v, ks[:, hs], (((0,), (0,)), ((), ())), preferred_element_type=f32)


def _scan_direct(q_ref, v_ref, k_ref, b_ref, o_ref, st_ref, *, reverse, chunk, heads):
    n = q_ref.shape[0]
    nc = n // chunk
    pos = lax.broadcasted_iota(jnp.int32, (chunk, 1), 0)

    def chunk_body(ci, carry):
        c = nc - 1 - ci if reverse else ci
        r0 = pl.multiple_of(c * chunk, chunk)
        rs = pl.ds(r0, chunk)

        def row_body(t, carry):
            b = b_ref[rs, :]
            diff = b_ref[pl.ds(r0 + t, 1), :] - b
            seen = (pos >= t) if reverse else (pos <= t)
            p = q_ref[pl.ds(r0 + t, 1), :] * k_ref[rs, :] * jnp.exp(jnp.where(seen, diff, -jnp.inf))
            outs = []
            for h in range(heads):
                hs = slice(h * HEAD_DIM, (h + 1) * HEAD_DIM)
                sc = jnp.sum(p[:, hs], axis=-1, keepdims=True)
                outs.append(jnp.sum(sc * v_ref[rs, hs].astype(f32), axis=0, keepdims=True))
            o_ref[pl.ds(r0 + t, 1), :] = jnp.concatenate(outs, axis=1)
            return carry

        lax.fori_loop(0, chunk, row_body, 0)
        b = b_ref[rs, :]
        b_end = b[:1] if reverse else b[chunk - 1:]
        e_end = jnp.exp(b_end)
        qs = (q_ref[rs, :] * jnp.exp(b)).astype(bf16)
        ks = (k_ref[rs, :] * jnp.exp(b_end - b)).astype(bf16)
        for h in range(heads):
            hs = slice(h * HEAD_DIM, (h + 1) * HEAD_DIM)
            st = st_ref[h]
            o_ref[rs, hs] += lax.dot_general(qs[:, hs], st.astype(bf16),
                                             (((1,), (1,)), ((), ())),
                                             preferred_element_type=f32)
            st_ref[h] = st * e_end[:, hs] + lax.dot_general(
                v_ref[rs, hs], ks[:, hs], (((0,), (0,)), ((), ())),
                preferred_element_type=f32)
        return carry

    lax.fori_loop(0, nc, chunk_body, 0)


def _scan_kernel(qf_ref, vf_ref, kf_ref, lff_ref, qb_ref, vb_ref, kb_ref, lfb_ref,
                 of_ref, ob_ref, stf_ref, stb_ref, bf_ref, bb_ref, *, chunk, heads):
    @pl.when(pl.program_id(1) == 0)
    def _():
        stf_ref[...] = jnp.zeros_like(stf_ref)
        stb_ref[...] = jnp.zeros_like(stb_ref)

    b_f = _chunk_cumsum(lff_ref[...], chunk, reverse=False)
    b_b = _chunk_cumsum(lfb_ref[...], chunk, reverse=True)
    bf_ref[...] = b_f
    bb_ref[...] = b_b
    decay = jnp.maximum(jnp.max(jnp.abs(b_f)), jnp.max(jnp.abs(b_b)))
    fwd = (qf_ref, vf_ref, kf_ref, bf_ref, of_ref, stf_ref)
    bwd = (qb_ref, vb_ref, kb_ref, bb_ref, ob_ref, stb_ref)

    tame = decay <= MAX_CHUNK_DECAY

    @pl.when(tame)
    def _():
        _scan_factorised(*fwd, reverse=False, chunk=chunk, heads=heads)
        _scan_factorised(*bwd, reverse=True, chunk=chunk, heads=heads)

    @pl.when(jnp.logical_not(tame))
    def _():
        _scan_direct(*fwd, reverse=False, chunk=chunk, heads=heads)
        _scan_direct(*bwd, reverse=True, chunk=chunk, heads=heads)


def _hgrn_scan(q, v, kf, lff, kb, lfb):
    b, s, d = q.shape
    blk = min(SCAN_BLOCK, s)
    chunk = min(SCAN_CHUNK, blk)
    nb = s // blk
    heads = d // HEAD_DIM
    fwd = pl.BlockSpec((None, blk, d), lambda bi, i: (bi, i, 0))
    bwd = pl.BlockSpec((None, blk, d), lambda bi, i: (bi, nb - 1 - i, 0))
    out = jax.ShapeDtypeStruct((b, s, d), f32)
    return pl.pallas_call(
        functools.partial(_scan_kernel, chunk=chunk, heads=heads),
        out_shape=(out, out),
        grid=(b, nb),
        in_specs=[fwd, fwd, fwd, fwd, bwd, bwd, bwd, bwd],
        out_specs=(fwd, bwd),
        scratch_shapes=[pltpu.VMEM((heads, HEAD_DIM, HEAD_DIM), f32),
                        pltpu.VMEM((heads, HEAD_DIM, HEAD_DIM), f32),
                        pltpu.VMEM((blk, d), f32),
                        pltpu.VMEM((blk, d), f32)],
        compiler_params=_params(2),
        name="hgrn_scan",
    )(q, v, kf, lff, q, v, kb, lfb)


def _hout_kernel(x_ref, of_ref, ob_ref, sg_ref, gn_ref, w_ref, o_ref, y_ref, *, heads):
    for h in range(heads):
        hs = slice(h * HEAD_DIM, (h + 1) * HEAD_DIM)
        o = of_ref[:, hs] + ob_ref[:, hs]
        o = o * lax.rsqrt(jnp.mean(o * o, axis=-1, keepdims=True) + EPS)
        y_ref[:, hs] = (o * gn_ref[:, hs] * sg_ref[:, hs]).astype(bf16)
    o_ref[...] = x_ref[...] + jnp.dot(y_ref[...], w_ref[...], preferred_element_type=f32)


def _hgrn_out(x, o_f, o_b, sg, gn, w_out):
    t, d = x.shape
    tm = min(TOKEN_TILE, t)
    row = pl.BlockSpec((tm, d), lambda i: (i, 0))
    return pl.pallas_call(
        functools.partial(_hout_kernel, heads=d // HEAD_DIM),
        out_shape=jax.ShapeDtypeStruct((t, d), f32),
        grid=(t // tm,),
        in_specs=[row, row, row, row, _resident((1, d)), _resident(w_out.shape)],
        out_specs=row,
        scratch_shapes=[pltpu.VMEM((tm, d), bf16)],
        compiler_params=_params(1),
        name="hgrn_out",
    )(x, o_f, o_b, sg, gn.reshape(1, d), w_out)


def kernel(x, norm_g, ffn_w13, ffn_w2, conv_w_pw1, conv_b_pw1, conv_w_dw, conv_b_dw,
           conv_ln_g, conv_ln_b, conv_w_pw2, conv_b_pw2, hgrn_w_in, hgrn_lb,
           hgrn_gn_g, hgrn_w_out, final_g):
    b, s, d = x.shape
    depth = norm_g.shape[0]
    t = b * s
    lb_table = jnp.cumsum(jax.nn.softmax(hgrn_lb.astype(f32), axis=1), axis=1)
    lb_table = lb_table - lb_table[:, :1]
    wb = lambda w: w.astype(bf16)

    x = x.reshape(t, d)
    for layer in range(depth):
        x = _ffn(x, norm_g[layer, 0], wb(ffn_w13[layer, 0]), wb(ffn_w2[layer, 0]),
                 final_g, final=False)
        j = layer // 2
        if layer % 2 == 0:
            x = _conv_mixer(x.reshape(b, s, d), norm_g[layer, 1], wb(conv_w_pw1[j]),
                            conv_b_pw1[j], conv_w_dw[j], conv_b_dw[j], conv_ln_g[j],
                            conv_ln_b[j], wb(conv_w_pw2[j]), conv_b_pw2[j]).reshape(t, d)
        else:
            q, v, kf, lff, kb, lfb, sg = _hgrn_proj(
                x, norm_g[layer, 1], wb(hgrn_w_in[j]), lb_table[0, layer], lb_table[1, layer])
            r3 = lambda a: a.reshape(b, s, d)
            o_f, o_b = _hgrn_scan(r3(q), r3(v), r3(kf), r3(lff), r3(kb), r3(lfb))
            x = _hgrn_out(x, o_f.reshape(t, d), o_b.reshape(t, d), sg, hgrn_gn_g[j],
                          wb(hgrn_w_out[j]))
        x = _ffn(x, norm_g[layer, 2], wb(ffn_w13[layer, 1]), wb(ffn_w2[layer, 1]),
                 final_g, final=(layer == depth - 1))
    return x.reshape(b, s, d)
```

```python
import functools

import jax
import jax.numpy as jnp
from jax import lax
from jax.experimental import pallas as pl
from jax.experimental.pallas import tpu as pltpu

EPS = 1e-6
FFN_RES = 0.5
CONV_WIDTH = 31
CONV_PAD = (CONV_WIDTH - 1) // 2
HEAD_DIM = 128
N_PROJ = 5

SUBLANES = 8
LANES = 128
MXU_COLS = 256
HALO = 2 * SUBLANES
VMEM_LIMIT = 56 * 1024 * 1024

TOKEN_TILE = 512
SCAN_BLOCK = 256
SCAN_CHUNK = 128
SCAN_LANES = 256
CONV_STRIP = 32
CONV_LANES = 512
MAX_CHUNK_DECAY = 80.0

f32 = jnp.float32
bf16 = jnp.bfloat16


def _rms(x, g):
    return x * lax.rsqrt(jnp.mean(x * x, axis=-1, keepdims=True) + EPS) * g


def _resident(shape):
    return pl.BlockSpec(shape, lambda *_: (0,) * len(shape), pipeline_mode=pl.Buffered(1))


def _params(n_axes):
    return pltpu.CompilerParams(dimension_semantics=("arbitrary",) * n_axes,
                                vmem_limit_bytes=VMEM_LIMIT)


def _ffn_kernel(*refs, d_ff, fc, final, heads):
    if heads:
        (x_ref, of_ref, ob_ref, sg_ref, gn_ref, wo_ref, g_ref, w13_ref, w2_ref, fg_ref,
         o_ref, a_ref, y_ref) = refs
        for h in range(heads):
            hs = slice(h * HEAD_DIM, (h + 1) * HEAD_DIM)
            o = of_ref[:, hs] + ob_ref[:, hs]
            o = o * lax.rsqrt(jnp.mean(o * o, axis=-1, keepdims=True) + EPS)
            y_ref[:, hs] = (o * gn_ref[:, hs] * sg_ref[:, hs]).astype(bf16)
        x = x_ref[...] + jnp.dot(y_ref[...], wo_ref[...], preferred_element_type=f32)
    else:
        x_ref, g_ref, w13_ref, w2_ref, fg_ref, o_ref, a_ref = refs
        x = x_ref[...]
    h = _rms(x, g_ref[...]).astype(bf16)
    for j in range(d_ff // fc):
        gate = jnp.dot(h, w13_ref[:, j * fc:(j + 1) * fc], preferred_element_type=f32)
        up = jnp.dot(h, w13_ref[:, d_ff + j * fc:d_ff + (j + 1) * fc],
                     preferred_element_type=f32)
        a_ref[:, j * fc:(j + 1) * fc] = (jax.nn.silu(gate) * up).astype(bf16)
    y = jnp.dot(a_ref[...], w2_ref[...], preferred_element_type=f32)
    out = x + FFN_RES * y
    if final:
        out = _rms(out, fg_ref[...])
    o_ref[...] = out


def _ffn(x, g, w13, w2, final_g, *, final, hgrn_out=None):
    t, d = x.shape
    d_ff = w2.shape[0]
    tm = min(TOKEN_TILE, t)
    fc = MXU_COLS if d_ff % MXU_COLS == 0 else d_ff
    row = pl.BlockSpec((tm, d), lambda i: (i, 0))
    args = [x]
    specs = [row]
    scratch = [pltpu.VMEM((tm, d_ff), bf16)]
    if hgrn_out is not None:
        o_f, o_b, sg, gn, w_out = hgrn_out
        args += [o_f, o_b, sg, gn.reshape(1, d), w_out]
        specs += [row, row, row, _resident((1, d)), _resident(w_out.shape)]
        scratch.append(pltpu.VMEM((tm, d), bf16))
    args += [g.reshape(1, d), w13, w2, final_g.reshape(1, d)]
    specs += [_resident((1, d)), _resident(w13.shape), _resident(w2.shape), _resident((1, d))]
    return pl.pallas_call(
        functools.partial(_ffn_kernel, d_ff=d_ff, fc=fc, final=final,
                          heads=0 if hgrn_out is None else d // HEAD_DIM),
        out_shape=jax.ShapeDtypeStruct((t, d), f32),
        grid=(t // tm,),
        in_specs=specs,
        out_specs=row,
        scratch_shapes=scratch,
        compiler_params=_params(1),
        name="ffn",
    )(*args)


def _conv_kernel(x_ref, xp_ref, xn_ref, g_ref, w1_ref, b1_ref, wdw_ref, bdw_ref,
                 lng_ref, lnb_ref, w2_ref, b2_ref, o_ref, u_ref, us_ref, c_ref, *, tm, d):
    i = pl.program_id(1)
    n = pl.num_programs(1)
    ext = tm + 2 * HALO
    xe = jnp.concatenate([xp_ref[...], x_ref[...], xn_ref[...]], axis=0)
    h = _rms(xe, g_ref[...]).astype(bf16)
    y = jnp.dot(h, w1_ref[...], preferred_element_type=f32) + b1_ref[...]
    u = y[:, :d] * jax.nn.sigmoid(y[:, d:])
    row = lax.broadcasted_iota(jnp.int32, (ext, 1), 0)
    inside = ((row >= HALO) | (i > 0)) & ((row < HALO + tm) | (i < n - 1))
    u_ref[...] = jnp.where(inside, u, 0.0)
    span = ext - SUBLANES
    for r in range(SUBLANES):
        us_ref[r] = u_ref[r:r + span, :]

    n_lane = d // CONV_LANES

    def strip(s, carry):
        r0 = pl.multiple_of(s * CONV_STRIP, CONV_STRIP)
        for c in range(n_lane):
            cs = slice(c * CONV_LANES, (c + 1) * CONV_LANES)
            acc = jnp.zeros((CONV_STRIP, CONV_LANES), f32)
            for k in range(CONV_WIDTH):
                off = k + HALO - CONV_PAD
                a, r = divmod(off, SUBLANES)
                tap = us_ref[r, pl.ds(r0 + a * SUBLANES, CONV_STRIP), cs]
                acc = acc + tap * wdw_ref[k:k + 1, cs]
            c_ref[pl.ds(r0, CONV_STRIP), cs] = acc
        return carry

    lax.fori_loop(0, tm // CONV_STRIP, strip, 0)

    cv = c_ref[...] + bdw_ref[...]
    mu = jnp.mean(cv, axis=-1, keepdims=True)
    cc = cv - mu
    var = jnp.mean(cc * cc, axis=-1, keepdims=True)
    z = cc * lax.rsqrt(var + EPS) * lng_ref[...] + lnb_ref[...]
    mix = jnp.dot(jax.nn.silu(z).astype(bf16), w2_ref[...],
                  preferred_element_type=f32) + b2_ref[...]
    o_ref[...] = x_ref[...] + mix


def _conv_mixer(x, g, w1, b1, wdw, bdw, lng, lnb, w2, b2):
    b, s, d = x.shape
    tm = min(TOKEN_TILE, s)
    nh = tm // HALO
    last = s // HALO - 1
    ext = tm + 2 * HALO
    main = pl.BlockSpec((None, tm, d), lambda bi, i: (bi, i, 0))
    prev = pl.BlockSpec((None, HALO, d), lambda bi, i: (bi, jnp.maximum(i * nh - 1, 0), 0))
    nxt = pl.BlockSpec((None, HALO, d), lambda bi, i: (bi, jnp.minimum((i + 1) * nh, last), 0))
    vec = lambda a: a.reshape(1, -1)
    return pl.pallas_call(
        functools.partial(_conv_kernel, tm=tm, d=d),
        out_shape=jax.ShapeDtypeStruct((b, s, d), f32),
        grid=(b, s // tm),
        in_specs=[main, prev, nxt, _resident((1, d)), _resident(w1.shape),
                  _resident((1, 2 * d)), _resident(wdw.shape), _resident((1, d)),
                  _resident((1, d)), _resident((1, d)), _resident(w2.shape),
                  _resident((1, d))],
        out_specs=main,
        scratch_shapes=[pltpu.VMEM((ext, d), f32),
                        pltpu.VMEM((SUBLANES, ext - SUBLANES, d), f32),
                        pltpu.VMEM((tm, d), f32)],
        compiler_params=_params(2),
        name="conv_mixer",
    )(x, x, x, vec(g), w1, vec(b1), wdw, vec(bdw), vec(lng), vec(lnb), w2, vec(b2))


def _hproj_kernel(x_ref, g_ref, w_ref, lbf_ref, lbb_ref,
                  q_ref, v_ref, lff_ref, lfb_ref, sg_ref, *, d):
    h = _rms(x_ref[...], g_ref[...]).astype(bf16)

    def proj(j):
        return jnp.dot(h, w_ref[:, j * d:(j + 1) * d], preferred_element_type=f32)

    q_ref[...] = jax.nn.silu(proj(0))
    v_ref[...] = proj(1).astype(bf16)
    for j, lb_ref, lf_ref in ((2, lbf_ref, lff_ref), (3, lbb_ref, lfb_ref)):
        lb = lb_ref[...]
        lf_ref[...] = jnp.log(lb + (1.0 - lb) * jax.nn.sigmoid(proj(j)))
    sg_ref[...] = jax.nn.silu(proj(4))


def _hgrn_proj(x, g, w_in, lb_f, lb_b):
    t, d = x.shape
    tm = min(TOKEN_TILE, t)
    row = pl.BlockSpec((tm, d), lambda i: (i, 0))
    sd = lambda dt: jax.ShapeDtypeStruct((t, d), dt)
    return pl.pallas_call(
        functools.partial(_hproj_kernel, d=d),
        out_shape=(sd(f32), sd(bf16), sd(f32), sd(f32), sd(f32)),
        grid=(t // tm,),
        in_specs=[row, _resident((1, d)), _resident(w_in.shape), _resident((1, d)),
                  _resident((1, d))],
        out_specs=(row,) * 5,
        compiler_params=_params(1),
        name="hgrn_proj",
    )(x, g.reshape(1, d), w_in, lb_f.reshape(1, d), lb_b.reshape(1, d))


def _chunk_cumsum(lf_ref, b_ref, chunk, reverse):
    n, d = lf_ref.shape
    sub = lax.broadcasted_iota(jnp.int32, (SUBLANES, 1), 0)
    peak = jnp.zeros((SUBLANES, SCAN_LANES), f32)
    for c in range(n // chunk):
        for l in range(d // SCAN_LANES):
            ls = slice(l * SCAN_LANES, (l + 1) * SCAN_LANES)
            tiles = range(chunk // SUBLANES)
            carry = None
            for g in (reversed(tiles) if reverse else tiles):
                rs = slice(c * chunk + g * SUBLANES, c * chunk + (g + 1) * SUBLANES)
                t = lf_ref[rs, ls]
                shift = 1
                while shift < SUBLANES:
                    if reverse:
                        moved = pltpu.roll(t, SUBLANES - shift, axis=0)
                        keep = sub < SUBLANES - shift
                    else:
                        moved = pltpu.roll(t, shift, axis=0)
                        keep = sub >= shift
                    t = t + jnp.where(keep, moved, 0.0)
                    shift *= 2
                if carry is not None:
                    t = t + carry
                b_ref[rs, ls] = t
                peak = jnp.maximum(peak, jnp.abs(t))
                end = t[:1] if reverse else t[SUBLANES - 1:]
                carry = jnp.broadcast_to(end, (SUBLANES, SCAN_LANES))
    return jnp.max(peak, keepdims=True)


def _scan_factorised(q_ref, v_ref, lf_ref, b_ref, o_ref, st_ref, *, reverse, chunk, heads):
    n = q_ref.shape[0]
    ti = lax.broadcasted_iota(jnp.int32, (chunk, chunk), 0)
    si = lax.broadcasted_iota(jnp.int32, (chunk, chunk), 1)
    causal = (si >= ti) if reverse else (si <= ti)
    order = range(n // chunk - 1, -1, -1) if reverse else range(n // chunk)
    for c in order:
        rs = slice(c * chunk, (c + 1) * chunk)
        b = b_ref[rs, :]
        b_end = b[:1] if reverse else b[chunk - 1:]
        mid = 0.5 * b_end
        e_mid = jnp.exp(mid)
        e_end = jnp.exp(b_end)
        qd = q_ref[rs, :] * jnp.exp(b - mid)
        kd = (1.0 - jnp.exp(lf_ref[rs, :])) * jnp.exp(mid - b)
        qs = (qd * e_mid).astype(bf16)
        ks = (kd * e_mid).astype(bf16)
        qd = qd.astype(bf16)
        kd = kd.astype(bf16)
        for h in range(heads):
            hs = slice(h * HEAD_DIM, (h + 1) * HEAD_DIM)
            v = v_ref[rs, hs]
            sc = lax.dot_general(qd[:, hs], kd[:, hs], (((1,), (1,)), ((), ())),
                                 preferred_element_type=f32)
            sc = jnp.where(causal, sc, 0.0).astype(bf16)
            st = st_ref[h]
            o_ref[rs, hs] = (jnp.dot(sc, v, preferred_element_type=f32)
                             + lax.dot_general(qs[:, hs], st.astype(bf16),
                                               (((1,), (1,)), ((), ())),
                                               preferred_element_type=f32))
            st_ref[h] = st * e_end[:, hs] + lax.dot_general(
                v, ks[:, hs], (((0,), (0,)), ((), ())), preferred_element_type=f32)


def _scan_direct(q_ref, v_ref, lf_ref, b_ref, o_ref, st_ref, *, reverse, chunk, heads):
    n = q_ref.shape[0]
    nc = n // chunk
    pos = lax.broadcasted_iota(jnp.int32, (chunk, 1), 0)

    def chunk_body(ci, carry):
        c = nc - 1 - ci if reverse else ci
        r0 = pl.multiple_of(c * chunk, chunk)
        rs = pl.ds(r0, chunk)

        def row_body(t, carry):
            b = b_ref[rs, :]
            diff = b_ref[pl.ds(r0 + t, 1), :] - b
            seen = (pos >= t) if reverse else (pos <= t)
            k = 1.0 - jnp.exp(lf_ref[rs, :])
            p = q_ref[pl.ds(r0 + t, 1), :] * k * jnp.exp(jnp.where(seen, diff, -jnp.inf))
            outs = []
            for h in range(heads):
                hs = slice(h * HEAD_DIM, (h + 1) * HEAD_DIM)
                sc = jnp.sum(p[:, hs], axis=-1, keepdims=True)
                outs.append(jnp.sum(sc * v_ref[rs, hs].astype(f32), axis=0, keepdims=True))
            o_ref[pl.ds(r0 + t, 1), :] = jnp.concatenate(outs, axis=1)
            return carry

        lax.fori_loop(0, chunk, row_body, 0)
        b = b_ref[rs, :]
        b_end = b[:1] if reverse else b[chunk - 1:]
        e_end = jnp.exp(b_end)
        qs = (q_ref[rs, :] * jnp.exp(b)).astype(bf16)
        ks = ((1.0 - jnp.exp(lf_ref[rs, :])) * jnp.exp(b_end - b)).astype(bf16)
        for h in range(heads):
            hs = slice(h * HEAD_DIM, (h + 1) * HEAD_DIM)
            st = st_ref[h]
            o_ref[rs, hs] += lax.dot_general(qs[:, hs], st.astype(bf16),
                                             (((1,), (1,)), ((), ())),
                                             preferred_element_type=f32)
            st_ref[h] = st * e_end[:, hs] + lax.dot_general(
                v_ref[rs, hs], ks[:, hs], (((0,), (0,)), ((), ())),
                preferred_element_type=f32)
        return carry

    lax.fori_loop(0, nc, chunk_body, 0)


def _scan_kernel(qf_ref, vf_ref, lff_ref, qb_ref, vb_ref, lfb_ref,
                 of_ref, ob_ref, stf_ref, stb_ref, bf_ref, bb_ref, *, chunk, heads):
    @pl.when(pl.program_id(1) == 0)
    def _():
        stf_ref[...] = jnp.zeros_like(stf_ref)
        stb_ref[...] = jnp.zeros_like(stb_ref)

    decay = jnp.maximum(_chunk_cumsum(lff_ref, bf_ref, chunk, reverse=False),
                        _chunk_cumsum(lfb_ref, bb_ref, chunk, reverse=True))[0, 0]
    fwd = (qf_ref, vf_ref, lff_ref, bf_ref, of_ref, stf_ref)
    bwd = (qb_ref, vb_ref, lfb_ref, bb_ref, ob_ref, stb_ref)

    tame = decay <= MAX_CHUNK_DECAY

    @pl.when(tame)
    def _():
        _scan_factorised(*fwd, reverse=False, chunk=chunk, heads=heads)
        _scan_factorised(*bwd, reverse=True, chunk=chunk, heads=heads)

    @pl.when(jnp.logical_not(tame))
    def _():
        _scan_direct(*fwd, reverse=False, chunk=chunk, heads=heads)
        _scan_direct(*bwd, reverse=True, chunk=chunk, heads=heads)


def _hgrn_scan(q, v, lff, lfb):
    b, s, d = q.shape
    blk = min(SCAN_BLOCK, s)
    chunk = min(SCAN_CHUNK, blk)
    nb = s // blk
    heads = d // HEAD_DIM
    fwd = pl.BlockSpec((None, blk, d), lambda bi, i: (bi, i, 0))
    bwd = pl.BlockSpec((None, blk, d), lambda bi, i: (bi, nb - 1 - i, 0))
    out = jax.ShapeDtypeStruct((b, s, d), f32)
    return pl.pallas_call(
        functools.partial(_scan_kernel, chunk=chunk, heads=heads),
        out_shape=(out, out),
        grid=(b, nb),
        in_specs=[fwd, fwd, fwd, bwd, bwd, bwd],
        out_specs=(fwd, bwd),
        scratch_shapes=[pltpu.VMEM((heads, HEAD_DIM, HEAD_DIM), f32),
                        pltpu.VMEM((heads, HEAD_DIM, HEAD_DIM), f32),
                        pltpu.VMEM((blk, d), f32),
                        pltpu.VMEM((blk, d), f32)],
        compiler_params=_params(2),
        name="hgrn_scan",
    )(q, v, lff, q, v, lfb)


def kernel(x, norm_g, ffn_w13, ffn_w2, conv_w_pw1, conv_b_pw1, conv_w_dw, conv_b_dw,
           conv_ln_g, conv_ln_b, conv_w_pw2, conv_b_pw2, hgrn_w_in, hgrn_lb,
           hgrn_gn_g, hgrn_w_out, final_g):
    b, s, d = x.shape
    depth = norm_g.shape[0]
    t = b * s
    lb_table = jnp.cumsum(jax.nn.softmax(hgrn_lb.astype(f32), axis=1), axis=1)
    lb_table = lb_table - lb_table[:, :1]
    wb = lambda w: w.astype(bf16)

    x = x.reshape(t, d)
    for layer in range(depth):
        x = _ffn(x, norm_g[layer, 0], wb(ffn_w13[layer, 0]), wb(ffn_w2[layer, 0]),
                 final_g, final=False)
        j = layer // 2
        hgrn_out = None
        if layer % 2 == 0:
            x = _conv_mixer(x.reshape(b, s, d), norm_g[layer, 1], wb(conv_w_pw1[j]),
                            conv_b_pw1[j], conv_w_dw[j], conv_b_dw[j], conv_ln_g[j],
                            conv_ln_b[j], wb(conv_w_pw2[j]), conv_b_pw2[j]).reshape(t, d)
        else:
            q, v, lff, lfb, sg = _hgrn_proj(
                x, norm_g[layer, 1], wb(hgrn_w_in[j]), lb_table[0, layer], lb_table[1, layer])
            r3 = lambda a: a.reshape(b, s, d)
            o_f, o_b = _hgrn_scan(r3(q), r3(v), r3(lff), r3(lfb))
            hgrn_out = (o_f.reshape(t, d), o_b.reshape(t, d), sg, hgrn_gn_g[j],
                        wb(hgrn_w_out[j]))
        x = _ffn(x, norm_g[layer, 2], wb(ffn_w13[layer, 1]), wb(ffn_w2[layer, 1]),
                 final_g, final=(layer == depth - 1), hgrn_out=hgrn_out)
    return x.reshape(b, s, d)
```

```python
import functools

import jax
import jax.numpy as jnp
import numpy as np
from jax import lax
from jax.experimental import pallas as pl
from jax.experimental.pallas import tpu as pltpu

EPS = 1e-6
FFN_RES = 0.5
CONV_WIDTH = 31
CONV_PAD = (CONV_WIDTH - 1) // 2
HEAD_DIM = 128
N_PROJ = 5

SUBLANES = 8
LANES = 128
MXU_COLS = 256
HALO = 2 * SUBLANES
VMEM_LIMIT = 56 * 1024 * 1024

TOKEN_TILE = 512
SCAN_BLOCK = 256
SCAN_CHUNK = 128
SCAN_LANES = 256
CONV_GROUP = 4
CONV_LANES = 512
MAX_CHUNK_DECAY = 80.0

f32 = jnp.float32
bf16 = jnp.bfloat16


def _rms(x, g):
    return x * lax.rsqrt(jnp.mean(x * x, axis=-1, keepdims=True) + EPS) * g


def _resident(shape):
    return pl.BlockSpec(shape, lambda *_: (0,) * len(shape), pipeline_mode=pl.Buffered(1))


def _params(n_axes):
    return pltpu.CompilerParams(dimension_semantics=("arbitrary",) * n_axes,
                                vmem_limit_bytes=VMEM_LIMIT)


def _ffn_kernel(*refs, d_ff, fc, final, heads):
    if heads:
        (x_ref, of_ref, ob_ref, sg_ref, gn_ref, wo_ref, g_ref, w13_ref, w2_ref, fg_ref,
         o_ref, a_ref, y_ref) = refs
        for h in range(heads):
            hs = slice(h * HEAD_DIM, (h + 1) * HEAD_DIM)
            o = of_ref[:, hs] + ob_ref[:, hs]
            o = o * lax.rsqrt(jnp.mean(o * o, axis=-1, keepdims=True) + EPS)
            y_ref[:, hs] = (o * gn_ref[:, hs] * sg_ref[:, hs]).astype(bf16)
        x = x_ref[...] + jnp.dot(y_ref[...], wo_ref[...], preferred_element_type=f32)
    else:
        x_ref, g_ref, w13_ref, w2_ref, fg_ref, o_ref, a_ref = refs
        x = x_ref[...]
    h = _rms(x, g_ref[...]).astype(bf16)
    for j in range(d_ff // fc):
        gate = jnp.dot(h, w13_ref[:, j * fc:(j + 1) * fc], preferred_element_type=f32)
        up = jnp.dot(h, w13_ref[:, d_ff + j * fc:d_ff + (j + 1) * fc],
                     preferred_element_type=f32)
        a_ref[:, j * fc:(j + 1) * fc] = (jax.nn.silu(gate) * up).astype(bf16)
    y = jnp.dot(a_ref[...], w2_ref[...], preferred_element_type=f32)
    out = x + FFN_RES * y
    if final:
        out = _rms(out, fg_ref[...])
    o_ref[...] = out


def _ffn(x, g, w13, w2, final_g, *, final, hgrn_out=None):
    t, d = x.shape
    d_ff = w2.shape[0]
    tm = min(TOKEN_TILE, t)
    fc = MXU_COLS if d_ff % MXU_COLS == 0 else d_ff
    row = pl.BlockSpec((tm, d), lambda i: (i, 0))
    args = [x]
    specs = [row]
    scratch = [pltpu.VMEM((tm, d_ff), bf16)]
    if hgrn_out is not None:
        o_f, o_b, sg, gn, w_out = hgrn_out
        args += [o_f, o_b, sg, gn.reshape(1, d), w_out]
        specs += [row, row, row, _resident((1, d)), _resident(w_out.shape)]
        scratch.append(pltpu.VMEM((tm, d), bf16))
    args += [g.reshape(1, d), w13, w2, final_g.reshape(1, d)]
    specs += [_resident((1, d)), _resident(w13.shape), _resident(w2.shape), _resident((1, d))]
    return pl.pallas_call(
        functools.partial(_ffn_kernel, d_ff=d_ff, fc=fc, final=final,
                          heads=0 if hgrn_out is None else d // HEAD_DIM),
        out_shape=jax.ShapeDtypeStruct((t, d), f32),
        grid=(t // tm,),
        in_specs=specs,
        out_specs=row,
        scratch_shapes=scratch,
        compiler_params=_params(1),
        name="ffn",
    )(*args)


def _interleave_matrices(tm):
    ext = tm + 2 * HALO
    p = ext // SUBLANES
    rho = np.arange(ext)
    to_il = np.zeros((ext, ext), np.float32)
    to_il[rho, (rho % SUBLANES) * p + rho // SUBLANES] = 1.0
    e = np.arange(tm) + HALO
    to_nat = np.zeros((tm, ext), np.float32)
    to_nat[np.arange(tm), (e % p) * SUBLANES + e // p] = 1.0
    return jnp.asarray(to_il, bf16), jnp.asarray(to_nat, bf16)


def _conv_kernel(x_ref, xp_ref, xn_ref, g_ref, pm_ref, qm_ref, w1_ref, b1_ref, wdw_ref, bdw_ref,
                 lng_ref, lnb_ref, w2_ref, b2_ref, o_ref, hp_ref, u_ref, c_ref, *, tm, d):
    i = pl.program_id(1)
    n = pl.num_programs(1)
    ext = tm + 2 * HALO
    p = ext // SUBLANES
    pad = CONV_PAD * SUBLANES
    xe = jnp.concatenate([xp_ref[...], x_ref[...], xn_ref[...]], axis=0)
    h = _rms(xe, g_ref[...]).astype(bf16)
    hp_ref[...] = jnp.dot(pm_ref[...], h, preferred_element_type=f32).astype(bf16)

    rho = lax.broadcasted_iota(jnp.int32, (ext, 1), 0)
    e = (rho % SUBLANES) * p + rho // SUBLANES
    inside = ((e >= HALO) | (i > 0)) & ((e < HALO + tm) | (i < n - 1))

    for cb in range(d // MXU_COLS):
        cs = slice(cb * MXU_COLS, (cb + 1) * MXU_COLS)
        gs = slice(d + cb * MXU_COLS, d + (cb + 1) * MXU_COLS)
        a = jnp.dot(hp_ref[...], w1_ref[:, cs], preferred_element_type=f32) + b1_ref[:, cs]
        gate = jnp.dot(hp_ref[...], w1_ref[:, gs], preferred_element_type=f32) + b1_ref[:, gs]
        u = jnp.where(inside, a * jax.nn.sigmoid(gate), 0.0)
        u_ref[pad:pad + ext, cs] = u
        tail = u[ext - pad:].reshape(CONV_PAD, SUBLANES, MXU_COLS)
        head = u[:pad].reshape(CONV_PAD, SUBLANES, MXU_COLS)
        u_ref[:pad, cs] = pltpu.roll(tail, 1, axis=1).reshape(pad, MXU_COLS)
        u_ref[pad + ext:, cs] = pltpu.roll(head, SUBLANES - 1, axis=1).reshape(pad, MXU_COLS)

    def group(gi, carry):
        base = pl.multiple_of(gi * (CONV_GROUP * SUBLANES), CONV_GROUP * SUBLANES)
        for c in range(d // CONV_LANES):
            cs = slice(c * CONV_LANES, (c + 1) * CONV_LANES)
            taps = {}
            acc = [None] * CONV_GROUP
            for m in range(CONV_GROUP + CONV_WIDTH - 1):
                rows = u_ref[pl.ds(base + m * SUBLANES, SUBLANES), cs]
                for q in range(CONV_GROUP):
                    k = m - q
                    if 0 <= k < CONV_WIDTH:
                        if k not in taps:
                            taps[k] = wdw_ref[k * SUBLANES:(k + 1) * SUBLANES, cs]
                        term = rows * taps[k]
                        acc[q] = term if acc[q] is None else acc[q] + term
            for q in range(CONV_GROUP):
                c_ref[pl.ds(base + q * SUBLANES, SUBLANES), cs] = acc[q] + bdw_ref[:, cs]
        return carry

    lax.fori_loop(0, p // CONV_GROUP, group, 0)

    cv = c_ref[...]
    mu = jnp.mean(cv, axis=-1, keepdims=True)
    cc = cv - mu
    var = jnp.mean(cc * cc, axis=-1, keepdims=True)
    z = cc * lax.rsqrt(var + EPS) * lng_ref[...] + lnb_ref[...]
    z = jax.nn.silu(z).astype(bf16)
    z = jnp.dot(qm_ref[...], z, preferred_element_type=f32).astype(bf16)
    mix = jnp.dot(z, w2_ref[...], preferred_element_type=f32) + b2_ref[...]
    o_ref[...] = x_ref[...] + mix


def _conv_mixer(x, g, w1, b1, wdw, bdw, lng, lnb, w2, b2):
    b, s, d = x.shape
    tm = min(TOKEN_TILE, s)
    nh = tm // HALO
    last = s // HALO - 1
    ext = tm + 2 * HALO
    assert ext % SUBLANES == 0 and (ext // SUBLANES) % CONV_GROUP == 0
    assert ext // SUBLANES >= CONV_PAD and HALO >= CONV_PAD
    pm, qm = _interleave_matrices(tm)
    wdw8 = jnp.repeat(wdw, SUBLANES, axis=0)
    main = pl.BlockSpec((None, tm, d), lambda bi, i: (bi, i, 0))
    prev = pl.BlockSpec((None, HALO, d), lambda bi, i: (bi, jnp.maximum(i * nh - 1, 0), 0))
    nxt = pl.BlockSpec((None, HALO, d), lambda bi, i: (bi, jnp.minimum((i + 1) * nh, last), 0))
    vec = lambda a: a.reshape(1, -1)
    return pl.pallas_call(
        functools.partial(_conv_kernel, tm=tm, d=d),
        out_shape=jax.ShapeDtypeStruct((b, s, d), f32),
        grid=(b, s // tm),
        in_specs=[main, prev, nxt, _resident((1, d)), _resident(pm.shape), _resident(qm.shape),
                  _resident(w1.shape), _resident((1, 2 * d)), _resident(wdw8.shape),
                  _resident((1, d)), _resident((1, d)), _resident((1, d)), _resident(w2.shape),
                  _resident((1, d))],
        out_specs=main,
        scratch_shapes=[pltpu.VMEM((ext, d), bf16),
                        pltpu.VMEM((ext + 2 * CONV_PAD * SUBLANES, d), f32),
                        pltpu.VMEM((ext, d), f32)],
        compiler_params=_params(2),
        name="conv_mixer",
    )(x, x, x, vec(g), pm, qm, w1, vec(b1), wdw8, vec(bdw), vec(lng), vec(lnb), w2, vec(b2))


def _hproj_kernel(x_ref, g_ref, w_ref, lbf_ref, lbb_ref,
                  q_ref, v_ref, lff_ref, lfb_ref, sg_ref, *, d):
    h = _rms(x_ref[...], g_ref[...]).astype(bf16)

    for c in range(d // MXU_COLS):
        cs = slice(c * MXU_COLS, (c + 1) * MXU_COLS)

        def proj(j):
            return jnp.dot(h, w_ref[:, j * d + c * MXU_COLS:j * d + (c + 1) * MXU_COLS],
                           preferred_element_type=f32)

        q_ref[:, cs] = jax.nn.silu(proj(0))
        v_ref[:, cs] = proj(1).astype(bf16)
        for j, lb_ref, lf_ref in ((2, lbf_ref, lff_ref), (3, lbb_ref, lfb_ref)):
            lb = lb_ref[:, cs]
            lf_ref[:, cs] = jnp.log(lb + (1.0 - lb) * jax.nn.sigmoid(proj(j)))
        sg_ref[:, cs] = jax.nn.silu(proj(4))


def _hgrn_proj(x, g, w_in, lb_f, lb_b):
    t, d = x.shape
    tm = min(TOKEN_TILE, t)
    row = pl.BlockSpec((tm, d), lambda i: (i, 0))
    sd = lambda dt: jax.ShapeDtypeStruct((t, d), dt)
    return pl.pallas_call(
        functools.partial(_hproj_kernel, d=d),
        out_shape=(sd(f32), sd(bf16), sd(f32), sd(f32), sd(f32)),
        grid=(t // tm,),
        in_specs=[row, _resident((1, d)), _resident(w_in.shape), _resident((1, d)),
                  _resident((1, d))],
        out_specs=(row,) * 5,
        compiler_params=_params(1),
        name="hgrn_proj",
    )(x, g.reshape(1, d), w_in, lb_f.reshape(1, d), lb_b.reshape(1, d))


def _chunk_cumsum(lf_ref, b_ref, chunk, reverse):
    n, d = lf_ref.shape
    sub = lax.broadcasted_iota(jnp.int32, (SUBLANES, 1), 0)
    peak = jnp.zeros((SUBLANES, SCAN_LANES), f32)
    for c in range(n // chunk):
        for l in range(d // SCAN_LANES):
            ls = slice(l * SCAN_LANES, (l + 1) * SCAN_LANES)
            tiles = range(chunk // SUBLANES)
            carry = None
            for g in (reversed(tiles) if reverse else tiles):
                rs = slice(c * chunk + g * SUBLANES, c * chunk + (g + 1) * SUBLANES)
                t = lf_ref[rs, ls]
                shift = 1
                while shift < SUBLANES:
                    if reverse:
                        moved = pltpu.roll(t, SUBLANES - shift, axis=0)
                        keep = sub < SUBLANES - shift
                    else:
                        moved = pltpu.roll(t, shift, axis=0)
                        keep = sub >= shift
                    t = t + jnp.where(keep, moved, 0.0)
                    shift *= 2
                if carry is not None:
                    t = t + carry
                b_ref[rs, ls] = t
                peak = jnp.maximum(peak, jnp.abs(t))
                end = t[:1] if reverse else t[SUBLANES - 1:]
                carry = jnp.broadcast_to(end, (SUBLANES, SCAN_LANES))
    return jnp.max(peak, keepdims=True)


def _scan_factorised(q_ref, v_ref, lf_ref, b_ref, o_ref, st_ref, *, reverse, chunk, heads):
    n = q_ref.shape[0]
    ti = lax.broadcasted_iota(jnp.int32, (chunk, chunk), 0)
    si = lax.broadcasted_iota(jnp.int32, (chunk, chunk), 1)
    causal = (si >= ti) if reverse else (si <= ti)
    order = range(n // chunk - 1, -1, -1) if reverse else range(n // chunk)
    for c in order:
        rs = slice(c * chunk, (c + 1) * chunk)
        b = b_ref[rs, :]
        b_end = b[:1] if reverse else b[chunk - 1:]
        mid = 0.5 * b_end
        e_mid = jnp.exp(mid)
        e_end = jnp.exp(b_end)
        qd = q_ref[rs, :] * jnp.exp(b - mid)
        kd = (1.0 - jnp.exp(lf_ref[rs, :])) * jnp.exp(mid - b)
        qs = (qd * e_mid).astype(bf16)
        ks = (kd * e_mid).astype(bf16)
        qd = qd.astype(bf16)
        kd = kd.astype(bf16)
        for h in range(heads):
            hs = slice(h * HEAD_DIM, (h + 1) * HEAD_DIM)
            v = v_ref[rs, hs]
            sc = lax.dot_general(qd[:, hs], kd[:, hs], (((1,), (1,)), ((), ())),
                                 preferred_element_type=f32)
            sc = jnp.where(causal, sc, 0.0).astype(bf16)
            st = st_ref[h]
            o_ref[rs, hs] = (jnp.dot(sc, v, preferred_element_type=f32)
                             + lax.dot_general(qs[:, hs], st.astype(bf16),
                                               (((1,), (1,)), ((), ())),
                                               preferred_element_type=f32))
            st_ref[h] = st * e_end[:, hs] + lax.dot_general(
                v, ks[:, hs], (((0,), (0,)), ((), ())), preferred_element_type=f32)


def _scan_direct(q_ref, v_ref, lf_ref, b_ref, o_ref, st_ref, *, reverse, chunk, heads):
    n = q_ref.shape[0]
    nc = n // chunk
    pos = lax.broadcasted_iota(jnp.int32, (chunk, 1), 0)

    def chunk_body(ci, carry):
        c = nc - 1 - ci if reverse else ci
        r0 = pl.multiple_of(c * chunk, chunk)
        rs = pl.ds(r0, chunk)

        def row_body(t, carry):
            b = b_ref[rs, :]
            diff = b_ref[pl.ds(r0 + t, 1), :] - b
            seen = (pos >= t) if reverse else (pos <= t)
            k = 1.0 - jnp.exp(lf_ref[rs, :])
            p = q_ref[pl.ds(r0 + t, 1), :] * k * jnp.exp(jnp.where(seen, diff, -jnp.inf))
            outs = []
            for h in range(heads):
                hs = slice(h * HEAD_DIM, (h + 1) * HEAD_DIM)
                sc = jnp.sum(p[:, hs], axis=-1, keepdims=True)
                outs.append(jnp.sum(sc * v_ref[rs, hs].astype(f32), axis=0, keepdims=True))
            o_ref[pl.ds(r0 + t, 1), :] = jnp.concatenate(outs, axis=1)
            return carry

        lax.fori_loop(0, chunk, row_body, 0)
        b = b_ref[rs, :]
        b_end = b[:1] if reverse else b[chunk - 1:]
        e_end = jnp.exp(b_end)
        qs = (q_ref[rs, :] * jnp.exp(b)).astype(bf16)
        ks = ((1.0 - jnp.exp(lf_ref[rs, :])) * jnp.exp(b_end - b)).astype(bf16)
        for h in range(heads):
            hs = slice(h * HEAD_DIM, (h + 1) * HEAD_DIM)
            st = st_ref[h]
            o_ref[rs, hs] += lax.dot_general(qs[:, hs], st.astype(bf16),
                                             (((1,), (1,)), ((), ())),
                                             preferred_element_type=f32)
            st_ref[h] = st * e_end[:, hs] + lax.dot_general(
                v_ref[rs, hs], ks[:, hs], (((0,), (0,)), ((), ())),
                preferred_element_type=f32)
        return carry

    lax.fori_loop(0, nc, chunk_body, 0)


def _scan_kernel(qf_ref, vf_ref, lff_ref, qb_ref, vb_ref, lfb_ref,
                 of_ref, ob_ref, stf_ref, stb_ref, bf_ref, bb_ref, *, chunk, heads):
    @pl.when(pl.program_id(1) == 0)
    def _():
        stf_ref[...] = jnp.zeros_like(stf_ref)
        stb_ref[...] = jnp.zeros_like(stb_ref)

    decay = jnp.maximum(_chunk_cumsum(lff_ref, bf_ref, chunk, reverse=False),
                        _chunk_cumsum(lfb_ref, bb_ref, chunk, reverse=True))[0, 0]
    fwd = (qf_ref, vf_ref, lff_ref, bf_ref, of_ref, stf_ref)
    bwd = (qb_ref, vb_ref, lfb_ref, bb_ref, ob_ref, stb_ref)

    tame = decay <= MAX_CHUNK_DECAY

    @pl.when(tame)
    def _():
        _scan_factorised(*fwd, reverse=False, chunk=chunk, heads=heads)
        _scan_factorised(*bwd, reverse=True, chunk=chunk, heads=heads)

    @pl.when(jnp.logical_not(tame))
    def _():
        _scan_direct(*fwd, reverse=False, chunk=chunk, heads=heads)
        _scan_direct(*bwd, reverse=True, chunk=chunk, heads=heads)


def _hgrn_scan(q, v, lff, lfb):
    b, s, d = q.shape
    blk = min(SCAN_BLOCK, s)
    chunk = min(SCAN_CHUNK, blk)
    nb = s // blk
    heads = d // HEAD_DIM
    fwd = pl.BlockSpec((None, blk, d), lambda bi, i: (bi, i, 0))
    bwd = pl.BlockSpec((None, blk, d), lambda bi, i: (bi, nb - 1 - i, 0))
    out = jax.ShapeDtypeStruct((b, s, d), f32)
    return pl.pallas_call(
        functools.partial(_scan_kernel, chunk=chunk, heads=heads),
        out_shape=(out, out),
        grid=(b, nb),
        in_specs=[fwd, fwd, fwd, bwd, bwd, bwd],
        out_specs=(fwd, bwd),
        scratch_shapes=[pltpu.VMEM((heads, HEAD_DIM, HEAD_DIM), f32),
                        pltpu.VMEM((heads, HEAD_DIM, HEAD_DIM), f32),
                        pltpu.VMEM((blk, d), f32),
                        pltpu.VMEM((blk, d), f32)],
        compiler_params=_params(2),
        name="hgrn_scan",
    )(q, v, lff, q, v, lfb)


def kernel(x, norm_g, ffn_w13, ffn_w2, conv_w_pw1, conv_b_pw1, conv_w_dw, conv_b_dw,
           conv_ln_g, conv_ln_b, conv_w_pw2, conv_b_pw2, hgrn_w_in, hgrn_lb,
           hgrn_gn_g, hgrn_w_out, final_g):
    b, s, d = x.shape
    depth = norm_g.shape[0]
    t = b * s
    lb_table = jnp.cumsum(jax.nn.softmax(hgrn_lb.astype(f32), axis=1), axis=1)
    lb_table = lb_table - lb_table[:, :1]
    wb = lambda w: w.astype(bf16)

    x = x.reshape(t, d)
    for layer in range(depth):
        x = _ffn(x, norm_g[layer, 0], wb(ffn_w13[layer, 0]), wb(ffn_w2[layer, 0]),
                 final_g, final=False)
        j = layer // 2
        hgrn_out = None
        if layer % 2 == 0:
            x = _conv_mixer(x.reshape(b, s, d), norm_g[layer, 1], wb(conv_w_pw1[j]),
                            conv_b_pw1[j], conv_w_dw[j], conv_b_dw[j], conv_ln_g[j],
                            conv_ln_b[j], wb(conv_w_pw2[j]), conv_b_pw2[j]).reshape(t, d)
        else:
            q, v, lff, lfb, sg = _hgrn_proj(
                x, norm_g[layer, 1], wb(hgrn_w_in[j]), lb_table[0, layer], lb_table[1, layer])
            r3 = lambda a: a.reshape(b, s, d)
            o_f, o_b = _hgrn_scan(r3(q), r3(v), r3(lff), r3(lfb))
            hgrn_out = (o_f.reshape(t, d), o_b.reshape(t, d), sg, hgrn_gn_g[j],
                        wb(hgrn_w_out[j]))
        x = _ffn(x, norm_g[layer, 2], wb(ffn_w13[layer, 1]), wb(ffn_w2[layer, 1]),
                 final_g, final=(layer == depth - 1), hgrn_out=hgrn_out)
    return x.reshape(b, s, d)
```

```python
import functools

import jax
import jax.numpy as jnp
import numpy as np
from jax import lax
from jax.experimental import pallas as pl
from jax.experimental.pallas import tpu as pltpu

EPS = 1e-6
FFN_RES = 0.5
CONV_WIDTH = 31
CONV_PAD = (CONV_WIDTH - 1) // 2
HEAD_DIM = 128
N_PROJ = 5

SUBLANES = 8
LANES = 128
MXU_COLS = 256
HALO = 2 * SUBLANES
VMEM_LIMIT = 56 * 1024 * 1024

TOKEN_TILE = 512
SCAN_BLOCK = 256
SCAN_CHUNK = 128
SCAN_LANES = 256
CONV_GROUP = 4
CONV_LANES = 512
MAX_CHUNK_DECAY = 80.0

f32 = jnp.float32
bf16 = jnp.bfloat16


def _rms(x, g):
    return x * lax.rsqrt(jnp.mean(x * x, axis=-1, keepdims=True) + EPS) * g


def _resident(shape):
    return pl.BlockSpec(shape, lambda *_: (0,) * len(shape), pipeline_mode=pl.Buffered(1))


def _params(n_axes):
    return pltpu.CompilerParams(dimension_semantics=("arbitrary",) * n_axes,
                                vmem_limit_bytes=VMEM_LIMIT)


def _ffn_kernel(*refs, d_ff, fc, final, heads):
    if heads:
        (x_ref, of_ref, ob_ref, sg_ref, gn_ref, wo_ref, g_ref, w13_ref, w2_ref, fg_ref,
         o_ref, a_ref, y_ref) = refs
        for h in range(heads):
            hs = slice(h * HEAD_DIM, (h + 1) * HEAD_DIM)
            o = of_ref[:, hs] + ob_ref[:, hs]
            o = o * lax.rsqrt(jnp.mean(o * o, axis=-1, keepdims=True) + EPS)
            y_ref[:, hs] = (o * gn_ref[:, hs] * sg_ref[:, hs]).astype(bf16)
        x = x_ref[...] + jnp.dot(y_ref[...], wo_ref[...], preferred_element_type=f32)
    else:
        x_ref, g_ref, w13_ref, w2_ref, fg_ref, o_ref, a_ref = refs
        x = x_ref[...]
    h = _rms(x, g_ref[...]).astype(bf16)
    for j in range(d_ff // fc):
        gate = jnp.dot(h, w13_ref[:, j * fc:(j + 1) * fc], preferred_element_type=f32)
        up = jnp.dot(h, w13_ref[:, d_ff + j * fc:d_ff + (j + 1) * fc],
                     preferred_element_type=f32)
        a_ref[:, j * fc:(j + 1) * fc] = (jax.nn.silu(gate) * up).astype(bf16)
    y = jnp.dot(a_ref[...], w2_ref[...], preferred_element_type=f32)
    out = x + FFN_RES * y
    if final:
        out = _rms(out, fg_ref[...])
    o_ref[...] = out


def _ffn(x, g, w13, w2, final_g, *, final, hgrn_out=None):
    t, d = x.shape
    d_ff = w2.shape[0]
    tm = min(TOKEN_TILE, t)
    fc = MXU_COLS if d_ff % MXU_COLS == 0 else d_ff
    row = pl.BlockSpec((tm, d), lambda i: (i, 0))
    args = [x]
    specs = [row]
    scratch = [pltpu.VMEM((tm, d_ff), bf16)]
    if hgrn_out is not None:
        o_f, o_b, sg, gn, w_out = hgrn_out
        args += [o_f, o_b, sg, gn.reshape(1, d), w_out]
        specs += [row, row, row, _resident((1, d)), _resident(w_out.shape)]
        scratch.append(pltpu.VMEM((tm, d), bf16))
    args += [g.reshape(1, d), w13, w2, final_g.reshape(1, d)]
    specs += [_resident((1, d)), _resident(w13.shape), _resident(w2.shape), _resident((1, d))]
    return pl.pallas_call(
        functools.partial(_ffn_kernel, d_ff=d_ff, fc=fc, final=final,
                          heads=0 if hgrn_out is None else d // HEAD_DIM),
        out_shape=jax.ShapeDtypeStruct((t, d), f32),
        grid=(t // tm,),
        in_specs=specs,
        out_specs=row,
        scratch_shapes=scratch,
        compiler_params=_params(1),
        name="ffn",
    )(*args)


def _interleave_matrices(tm):
    ext = tm + 2 * HALO
    p = ext // SUBLANES
    rho = np.arange(ext)
    to_il = np.zeros((ext, ext), np.float32)
    to_il[rho, (rho % SUBLANES) * p + rho // SUBLANES] = 1.0
    e = np.arange(tm) + HALO
    to_nat = np.zeros((tm, ext), np.float32)
    to_nat[np.arange(tm), (e % p) * SUBLANES + e // p] = 1.0
    return jnp.asarray(to_il, bf16), jnp.asarray(to_nat, bf16)


def _conv_kernel(x_ref, xp_ref, xn_ref, g_ref, pm_ref, qm_ref, w1_ref, b1_ref, wdw_ref, bdw_ref,
                 lng_ref, lnb_ref, w2_ref, b2_ref, o_ref, hp_ref, u_ref, c_ref, *, tm, d):
    i = pl.program_id(1)
    n = pl.num_programs(1)
    ext = tm + 2 * HALO
    p = ext // SUBLANES
    pad = CONV_PAD * SUBLANES
    xe = jnp.concatenate([xp_ref[...], x_ref[...], xn_ref[...]], axis=0)
    h = _rms(xe, g_ref[...]).astype(bf16)
    hp_ref[...] = jnp.dot(pm_ref[...], h, preferred_element_type=f32).astype(bf16)

    rho = lax.broadcasted_iota(jnp.int32, (ext, 1), 0)
    e = (rho % SUBLANES) * p + rho // SUBLANES
    inside = ((e >= HALO) | (i > 0)) & ((e < HALO + tm) | (i < n - 1))

    for cb in range(d // MXU_COLS):
        cs = slice(cb * MXU_COLS, (cb + 1) * MXU_COLS)
        gs = slice(d + cb * MXU_COLS, d + (cb + 1) * MXU_COLS)
        a = jnp.dot(hp_ref[...], w1_ref[:, cs], preferred_element_type=f32) + b1_ref[:, cs]
        gate = jnp.dot(hp_ref[...], w1_ref[:, gs], preferred_element_type=f32) + b1_ref[:, gs]
        u = jnp.where(inside, a * jax.nn.sigmoid(gate), 0.0)
        u_ref[pad:pad + ext, cs] = u
        tail = u[ext - pad:].reshape(CONV_PAD, SUBLANES, MXU_COLS)
        head = u[:pad].reshape(CONV_PAD, SUBLANES, MXU_COLS)
        u_ref[:pad, cs] = pltpu.roll(tail, 1, axis=1).reshape(pad, MXU_COLS)
        u_ref[pad + ext:, cs] = pltpu.roll(head, SUBLANES - 1, axis=1).reshape(pad, MXU_COLS)

    def group(gi, carry):
        base = pl.multiple_of(gi * (CONV_GROUP * SUBLANES), CONV_GROUP * SUBLANES)
        for c in range(d // CONV_LANES):
            cs = slice(c * CONV_LANES, (c + 1) * CONV_LANES)
            taps = {}
            acc = [None] * CONV_GROUP
            for m in range(CONV_GROUP + CONV_WIDTH - 1):
                rows = u_ref[pl.ds(base + m * SUBLANES, SUBLANES), cs]
                for q in range(CONV_GROUP):
                    k = m - q
                    if 0 <= k < CONV_WIDTH:
                        if k not in taps:
                            taps[k] = wdw_ref[k * SUBLANES:(k + 1) * SUBLANES, cs]
                        term = rows * taps[k]
                        acc[q] = term if acc[q] is None else acc[q] + term
            for q in range(CONV_GROUP):
                c_ref[pl.ds(base + q * SUBLANES, SUBLANES), cs] = acc[q] + bdw_ref[:, cs]
        return carry

    lax.fori_loop(0, p // CONV_GROUP, group, 0)

    cv = c_ref[...]
    mu = jnp.mean(cv, axis=-1, keepdims=True)
    cc = cv - mu
    var = jnp.mean(cc * cc, axis=-1, keepdims=True)
    z = cc * lax.rsqrt(var + EPS) * lng_ref[...] + lnb_ref[...]
    z = jax.nn.silu(z).astype(bf16)
    z = jnp.dot(qm_ref[...], z, preferred_element_type=f32).astype(bf16)
    mix = jnp.dot(z, w2_ref[...], preferred_element_type=f32) + b2_ref[...]
    o_ref[...] = x_ref[...] + mix


def _conv_mixer(x, g, w1, b1, wdw, bdw, lng, lnb, w2, b2):
    b, s, d = x.shape
    tm = min(TOKEN_TILE, s)
    nh = tm // HALO
    last = s // HALO - 1
    ext = tm + 2 * HALO
    assert ext % SUBLANES == 0 and (ext // SUBLANES) % CONV_GROUP == 0
    assert ext // SUBLANES >= CONV_PAD and HALO >= CONV_PAD
    pm, qm = _interleave_matrices(tm)
    wdw8 = jnp.repeat(wdw, SUBLANES, axis=0)
    main = pl.BlockSpec((None, tm, d), lambda bi, i: (bi, i, 0))
    prev = pl.BlockSpec((None, HALO, d), lambda bi, i: (bi, jnp.maximum(i * nh - 1, 0), 0))
    nxt = pl.BlockSpec((None, HALO, d), lambda bi, i: (bi, jnp.minimum((i + 1) * nh, last), 0))
    vec = lambda a: a.reshape(1, -1)
    return pl.pallas_call(
        functools.partial(_conv_kernel, tm=tm, d=d),
        out_shape=jax.ShapeDtypeStruct((b, s, d), f32),
        grid=(b, s // tm),
        in_specs=[main, prev, nxt, _resident((1, d)), _resident(pm.shape), _resident(qm.shape),
                  _resident(w1.shape), _resident((1, 2 * d)), _resident(wdw8.shape),
                  _resident((1, d)), _resident((1, d)), _resident((1, d)), _resident(w2.shape),
                  _resident((1, d))],
        out_specs=main,
        scratch_shapes=[pltpu.VMEM((ext, d), bf16),
                        pltpu.VMEM((ext + 2 * CONV_PAD * SUBLANES, d), f32),
                        pltpu.VMEM((ext, d), f32)],
        compiler_params=_params(2),
        name="conv_mixer",
    )(x, x, x, vec(g), pm, qm, w1, vec(b1), wdw8, vec(bdw), vec(lng), vec(lnb), w2, vec(b2))


def _hproj_kernel(x_ref, g_ref, w_ref, lbf_ref, lbb_ref,
                  q_ref, v_ref, lff_ref, lfb_ref, sg_ref, *, d):
    h = _rms(x_ref[...], g_ref[...]).astype(bf16)

    for c in range(d // MXU_COLS):
        cs = slice(c * MXU_COLS, (c + 1) * MXU_COLS)

        def proj(j):
            return jnp.dot(h, w_ref[:, j * d + c * MXU_COLS:j * d + (c + 1) * MXU_COLS],
                           preferred_element_type=f32)

        q_ref[:, cs] = jax.nn.silu(proj(0))
        v_ref[:, cs] = proj(1).astype(bf16)
        for j, lb_ref, lf_ref in ((2, lbf_ref, lff_ref), (3, lbb_ref, lfb_ref)):
            lb = lb_ref[:, cs]
            lf_ref[:, cs] = jnp.log(lb + (1.0 - lb) * jax.nn.sigmoid(proj(j)))
        sg_ref[:, cs] = jax.nn.silu(proj(4))


def _hgrn_proj(x, g, w_in, lb_f, lb_b):
    t, d = x.shape
    tm = min(TOKEN_TILE, t)
    row = pl.BlockSpec((tm, d), lambda i: (i, 0))
    sd = lambda dt: jax.ShapeDtypeStruct((t, d), dt)
    return pl.pallas_call(
        functools.partial(_hproj_kernel, d=d),
        out_shape=(sd(f32), sd(bf16), sd(f32), sd(f32), sd(f32)),
        grid=(t // tm,),
        in_specs=[row, _resident((1, d)), _resident(w_in.shape), _resident((1, d)),
                  _resident((1, d))],
        out_specs=(row,) * 5,
        compiler_params=_params(1),
        name="hgrn_proj",
    )(x, g.reshape(1, d), w_in, lb_f.reshape(1, d), lb_b.reshape(1, d))


def _chunk_cumsum(lf_ref, b_ref, k_ref, chunk, reverse):
    n, d = lf_ref.shape
    sub = lax.broadcasted_iota(jnp.int32, (SUBLANES, 1), 0)
    peak = jnp.zeros((SUBLANES, SCAN_LANES), f32)
    for c in range(n // chunk):
        for l in range(d // SCAN_LANES):
            ls = slice(l * SCAN_LANES, (l + 1) * SCAN_LANES)
            tiles = range(chunk // SUBLANES)
            carry = None
            for g in (reversed(tiles) if reverse else tiles):
                rs = slice(c * chunk + g * SUBLANES, c * chunk + (g + 1) * SUBLANES)
                t = lf_ref[rs, ls]
                k_ref[rs, ls] = 1.0 - jnp.exp(t)
                shift = 1
                while shift < SUBLANES:
                    if reverse:
                        moved = pltpu.roll(t, SUBLANES - shift, axis=0)
                        keep = sub < SUBLANES - shift
                    else:
                        moved = pltpu.roll(t, shift, axis=0)
                        keep = sub >= shift
                    t = t + jnp.where(keep, moved, 0.0)
                    shift *= 2
                if carry is not None:
                    t = t + carry
                b_ref[rs, ls] = t
                peak = jnp.maximum(peak, jnp.abs(t))
                end = t[:1] if reverse else t[SUBLANES - 1:]
                carry = jnp.broadcast_to(end, (SUBLANES, SCAN_LANES))
    return jnp.max(peak, keepdims=True)


def _scan_factorised(q_ref, v_ref, k_ref, b_ref, o_ref, st_ref, *, reverse, chunk, heads):
    n = q_ref.shape[0]
    ti = lax.broadcasted_iota(jnp.int32, (chunk, chunk), 0)
    si = lax.broadcasted_iota(jnp.int32, (chunk, chunk), 1)
    causal = (si >= ti) if reverse else (si <= ti)
    order = range(n // chunk - 1, -1, -1) if reverse else range(n // chunk)
    for c in order:
        rs = slice(c * chunk, (c + 1) * chunk)
        b = b_ref[rs, :]
        b_end = b[:1] if reverse else b[chunk - 1:]
        mid = 0.5 * b_end
        e_mid = jnp.exp(mid)
        e_end = jnp.exp(b_end)
        qd = q_ref[rs, :] * jnp.exp(b - mid)
        kd = k_ref[rs, :] * jnp.exp(mid - b)
        qs = (qd * e_mid).astype(bf16)
        ks = (kd * e_mid).astype(bf16)
        qd = qd.astype(bf16)
        kd = kd.astype(bf16)
        for h in range(heads):
            hs = slice(h * HEAD_DIM, (h + 1) * HEAD_DIM)
            v = v_ref[rs, hs]
            sc = lax.dot_general(qd[:, hs], kd[:, hs], (((1,), (1,)), ((), ())),
                                 preferred_element_type=f32)
            sc = jnp.where(causal, sc, 0.0).astype(bf16)
            st = st_ref[h]
            o_ref[rs, hs] = (jnp.dot(sc, v, preferred_element_type=f32)
                             + lax.dot_general(qs[:, hs], st.astype(bf16),
                                               (((1,), (1,)), ((), ())),
                                               preferred_element_type=f32))
            st_ref[h] = st * e_end[:, hs] + lax.dot_general(
                v, ks[:, hs], (((0,), (0,)), ((), ())), preferred_element_type=f32)


def _scan_direct(q_ref, v_ref, k_ref, b_ref, o_ref, st_ref, *, reverse, chunk, heads):
    n = q_ref.shape[0]
    nc = n // chunk
    pos = lax.broadcasted_iota(jnp.int32, (chunk, 1), 0)

    def chunk_body(ci, carry):
        c = nc - 1 - ci if reverse else ci
        r0 = pl.multiple_of(c * chunk, chunk)
        rs = pl.ds(r0, chunk)

        def row_body(t, carry):
            b = b_ref[rs, :]
            diff = b_ref[pl.ds(r0 + t, 1), :] - b
            seen = (pos >= t) if reverse else (pos <= t)
            p = (q_ref[pl.ds(r0 + t, 1), :] * k_ref[rs, :]
                 * jnp.exp(jnp.where(seen, diff, -jnp.inf)))
            outs = []
            for h in range(heads):
                hs = slice(h * HEAD_DIM, (h + 1) * HEAD_DIM)
                sc = jnp.sum(p[:, hs], axis=-1, keepdims=True)
                outs.append(jnp.sum(sc * v_ref[rs, hs].astype(f32), axis=0, keepdims=True))
            o_ref[pl.ds(r0 + t, 1), :] = jnp.concatenate(outs, axis=1)
            return carry

        lax.fori_loop(0, chunk, row_body, 0)
        b = b_ref[rs, :]
        b_end = b[:1] if reverse else b[chunk - 1:]
        e_end = jnp.exp(b_end)
        qs = (q_ref[rs, :] * jnp.exp(b)).astype(bf16)
        ks = (k_ref[rs, :] * jnp.exp(b_end - b)).astype(bf16)
        for h in range(heads):
            hs = slice(h * HEAD_DIM, (h + 1) * HEAD_DIM)
            st = st_ref[h]
            o_ref[rs, hs] += lax.dot_general(qs[:, hs], st.astype(bf16),
                                             (((1,), (1,)), ((), ())),
                                             preferred_element_type=f32)
            st_ref[h] = st * e_end[:, hs] + lax.dot_general(
                v_ref[rs, hs], ks[:, hs], (((0,), (0,)), ((), ())),
                preferred_element_type=f32)
        return carry

    lax.fori_loop(0, nc, chunk_body, 0)


def _scan_kernel(qf_ref, vf_ref, lff_ref, lff0_ref, qb_ref, vb_ref, lfb_ref, lfb0_ref,
                 of_ref, ob_ref, stf_ref, stb_ref, bf_ref, bb_ref, kf_ref, kb_ref, tame_ref,
                 *, chunk, heads):
    i = pl.program_id(1)
    slot = i % 2
    nxt = 1 - slot

    def prepare(lf_f, lf_b, dst):
        decay = jnp.maximum(
            _chunk_cumsum(lf_f, bf_ref.at[dst], kf_ref.at[dst], chunk, reverse=False),
            _chunk_cumsum(lf_b, bb_ref.at[dst], kb_ref.at[dst], chunk, reverse=True))[0, 0]
        tame_ref[dst] = (decay <= MAX_CHUNK_DECAY).astype(jnp.int32)

    @pl.when(i == 0)
    def _():
        stf_ref[...] = jnp.zeros_like(stf_ref)
        stb_ref[...] = jnp.zeros_like(stb_ref)
        prepare(lff0_ref, lfb0_ref, slot)

    fwd = (qf_ref, vf_ref, kf_ref.at[slot], bf_ref.at[slot], of_ref, stf_ref)
    bwd = (qb_ref, vb_ref, kb_ref.at[slot], bb_ref.at[slot], ob_ref, stb_ref)
    tame = tame_ref[slot] == 1

    @pl.when(tame)
    def _():
        _scan_factorised(*fwd, reverse=False, chunk=chunk, heads=heads)
        _scan_factorised(*bwd, reverse=True, chunk=chunk, heads=heads)
        prepare(lff_ref, lfb_ref, nxt)

    @pl.when(jnp.logical_not(tame))
    def _():
        _scan_direct(*fwd, reverse=False, chunk=chunk, heads=heads)
        _scan_direct(*bwd, reverse=True, chunk=chunk, heads=heads)
        prepare(lff_ref, lfb_ref, nxt)


def _hgrn_scan(q, v, lff, lfb):
    b, s, d = q.shape
    blk = min(SCAN_BLOCK, s)
    chunk = min(SCAN_CHUNK, blk)
    nb = s // blk
    heads = d // HEAD_DIM
    fwd = pl.BlockSpec((None, blk, d), lambda bi, i: (bi, i, 0))
    bwd = pl.BlockSpec((None, blk, d), lambda bi, i: (bi, nb - 1 - i, 0))
    fwd_next = pl.BlockSpec((None, blk, d), lambda bi, i: (bi, jnp.minimum(i + 1, nb - 1), 0))
    bwd_next = pl.BlockSpec((None, blk, d), lambda bi, i: (bi, jnp.maximum(nb - 2 - i, 0), 0))
    fwd_first = pl.BlockSpec((None, blk, d), lambda bi, i: (bi, 0, 0))
    bwd_first = pl.BlockSpec((None, blk, d), lambda bi, i: (bi, nb - 1, 0))
    out = jax.ShapeDtypeStruct((b, s, d), f32)
    state = pltpu.VMEM((heads, HEAD_DIM, HEAD_DIM), f32)
    slots = pltpu.VMEM((2, blk, d), f32)
    return pl.pallas_call(
        functools.partial(_scan_kernel, chunk=chunk, heads=heads),
        out_shape=(out, out),
        grid=(b, nb),
        in_specs=[fwd, fwd, fwd_next, fwd_first, bwd, bwd, bwd_next, bwd_first],
        out_specs=(fwd, bwd),
        scratch_shapes=[state, state, slots, slots, slots, slots, pltpu.SMEM((2,), jnp.int32)],
        compiler_params=_params(2),
        name="hgrn_scan",
    )(q, v, lff, lff, q, v, lfb, lfb)


def kernel(x, norm_g, ffn_w13, ffn_w2, conv_w_pw1, conv_b_pw1, conv_w_dw, conv_b_dw,
           conv_ln_g, conv_ln_b, conv_w_pw2, conv_b_pw2, hgrn_w_in, hgrn_lb,
           hgrn_gn_g, hgrn_w_out, final_g):
    b, s, d = x.shape
    depth = norm_g.shape[0]
    t = b * s
    lb_table = jnp.cumsum(jax.nn.softmax(hgrn_lb.astype(f32), axis=1), axis=1)
    lb_table = lb_table - lb_table[:, :1]
    wb = lambda w: w.astype(bf16)

    x = x.reshape(t, d)
    for layer in range(depth):
        x = _ffn(x, norm_g[layer, 0], wb(ffn_w13[layer, 0]), wb(ffn_w2[layer, 0]),
                 final_g, final=False)
        j = layer // 2
        hgrn_out = None
        if layer % 2 == 0:
            x = _conv_mixer(x.reshape(b, s, d), norm_g[layer, 1], wb(conv_w_pw1[j]),
                            conv_b_pw1[j], conv_w_dw[j], conv_b_dw[j], conv_ln_g[j],
                            conv_ln_b[j], wb(conv_w_pw2[j]), conv_b_pw2[j]).reshape(t, d)
        else:
            q, v, lff, lfb, sg = _hgrn_proj(
                x, norm_g[layer, 1], wb(hgrn_w_in[j]), lb_table[0, layer], lb_table[1, layer])
            r3 = lambda a: a.reshape(b, s, d)
            o_f, o_b = _hgrn_scan(r3(q), r3(v), r3(lff), r3(lfb))
            hgrn_out = (o_f.reshape(t, d), o_b.reshape(t, d), sg, hgrn_gn_g[j],
                        wb(hgrn_w_out[j]))
        x = _ffn(x, norm_g[layer, 2], wb(ffn_w13[layer, 1]), wb(ffn_w2[layer, 1]),
                 final_g, final=(layer == depth - 1), hgrn_out=hgrn_out)
    return x.reshape(b, s, d)
```

```python
import functools

import jax
import jax.numpy as jnp
import numpy as np
from jax import lax
from jax.experimental import pallas as pl
from jax.experimental.pallas import tpu as pltpu

EPS = 1e-6
FFN_RES = 0.5
CONV_WIDTH = 31
CONV_PAD = (CONV_WIDTH - 1) // 2
HEAD_DIM = 128
N_PROJ = 5

SUBLANES = 8
LANES = 128
MXU_COLS = 256
HALO = 2 * SUBLANES
VMEM_LIMIT = 56 * 1024 * 1024

TOKEN_TILE = 512
SCAN_BLOCK = 256
SCAN_CHUNK = 128
SCAN_LANES = 256
CONV_GROUP = 4
CONV_LANES = 512
MAX_CHUNK_DECAY = 80.0

f32 = jnp.float32
bf16 = jnp.bfloat16


def _rms(x, g):
    return x * lax.rsqrt(jnp.mean(x * x, axis=-1, keepdims=True) + EPS) * g


def _resident(shape):
    return pl.BlockSpec(shape, lambda *_: (0,) * len(shape), pipeline_mode=pl.Buffered(1))


def _params(n_axes):
    return pltpu.CompilerParams(dimension_semantics=("arbitrary",) * n_axes,
                                vmem_limit_bytes=VMEM_LIMIT)


def _ffn_kernel(*refs, d_ff, fc, final, heads):
    if heads:
        (x_ref, of_ref, ob_ref, sg_ref, gn_ref, wo_ref, g_ref, w13_ref, w2_ref, fg_ref,
         o_ref, a_ref, y_ref) = refs
        for h in range(heads):
            hs = slice(h * HEAD_DIM, (h + 1) * HEAD_DIM)
            o = of_ref[:, hs] + ob_ref[:, hs]
            o = o * lax.rsqrt(jnp.mean(o * o, axis=-1, keepdims=True) + EPS)
            y_ref[:, hs] = (o * gn_ref[:, hs] * sg_ref[:, hs]).astype(bf16)
        x = x_ref[...] + jnp.dot(y_ref[...], wo_ref[...], preferred_element_type=f32)
    else:
        x_ref, g_ref, w13_ref, w2_ref, fg_ref, o_ref, a_ref = refs
        x = x_ref[...]
    h = _rms(x, g_ref[...]).astype(bf16)
    for j in range(d_ff // fc):
        gate = jnp.dot(h, w13_ref[:, j * fc:(j + 1) * fc], preferred_element_type=f32)
        up = jnp.dot(h, w13_ref[:, d_ff + j * fc:d_ff + (j + 1) * fc],
                     preferred_element_type=f32)
        a_ref[:, j * fc:(j + 1) * fc] = (jax.nn.silu(gate) * up).astype(bf16)
    y = jnp.dot(a_ref[...], w2_ref[...], preferred_element_type=f32)
    out = x + FFN_RES * y
    if final:
        out = _rms(out, fg_ref[...])
    o_ref[...] = out


def _ffn(x, g, w13, w2, final_g, *, final, hgrn_out=None):
    t, d = x.shape
    d_ff = w2.shape[0]
    tm = min(TOKEN_TILE, t)
    fc = MXU_COLS if d_ff % MXU_COLS == 0 else d_ff
    row = pl.BlockSpec((tm, d), lambda i: (i, 0))
    args = [x]
    specs = [row]
    scratch = [pltpu.VMEM((tm, d_ff), bf16)]
    if hgrn_out is not None:
        o_f, o_b, sg, gn, w_out = hgrn_out
        args += [o_f, o_b, sg, gn.reshape(1, d), w_out]
        specs += [row, row, row, _resident((1, d)), _resident(w_out.shape)]
        scratch.append(pltpu.VMEM((tm, d), bf16))
    args += [g.reshape(1, d), w13, w2, final_g.reshape(1, d)]
    specs += [_resident((1, d)), _resident(w13.shape), _resident(w2.shape), _resident((1, d))]
    return pl.pallas_call(
        functools.partial(_ffn_kernel, d_ff=d_ff, fc=fc, final=final,
                          heads=0 if hgrn_out is None else d // HEAD_DIM),
        out_shape=jax.ShapeDtypeStruct((t, d), f32),
        grid=(t // tm,),
        in_specs=specs,
        out_specs=row,
        scratch_shapes=scratch,
        compiler_params=_params(1),
        name="ffn",
    )(*args)


def _interleave_matrices(tm):
    ext = tm + 2 * HALO
    p = ext // SUBLANES
    rho = np.arange(ext)
    to_il = np.zeros((ext, ext), np.float32)
    to_il[rho, (rho % SUBLANES) * p + rho // SUBLANES] = 1.0
    e = np.arange(tm) + HALO
    to_nat = np.zeros((tm, ext), np.float32)
    to_nat[np.arange(tm), (e % p) * SUBLANES + e // p] = 1.0
    return jnp.asarray(to_il, bf16), jnp.asarray(to_nat, bf16)


def _split_k_dot(m_ref, rhs):
    k0 = min(MXU_COLS, rhs.shape[0] // 2)
    return (jnp.dot(m_ref[:, :k0], rhs[:k0], preferred_element_type=f32)
            + jnp.dot(m_ref[:, k0:], rhs[k0:], preferred_element_type=f32))


def _conv_kernel(x_ref, xr_ref, xm_ref, xp_ref, xn_ref, g_ref, pm_ref, qm_ref, w1_ref, b1_ref,
                 wdw_ref, bdw_ref, lng_ref, lnb_ref, w2_ref, b2_ref, o_ref, hp_ref, y_ref, u_ref,
                 c_ref, *, tm, d):
    i = pl.program_id(1)
    n = pl.num_programs(1) - 1
    slot = i % 2
    ext = tm + 2 * HALO
    p = ext // SUBLANES
    pad = CONV_PAD * SUBLANES
    n_cb = d // MXU_COLS
    groups = p // CONV_GROUP

    def interleaved_input(main_ref, prev_rows, next_rows):
        xe = jnp.concatenate([prev_rows, main_ref[...], next_rows], axis=0)
        h = _rms(xe, g_ref[...]).astype(bf16)
        hp_ref[...] = _split_k_dot(pm_ref, h).astype(bf16)

    def pointwise_block(dst, tb):
        y_ref[dst, tb] = (jnp.dot(hp_ref[...], w1_ref[tb], preferred_element_type=f32)
                          + b1_ref[tb])

    @pl.when(i == 0)
    def _():
        interleaved_input(x_ref, xp_ref[...], xm_ref[:HALO, :])
        for tb in range(2 * n_cb):
            pointwise_block(slot, tb)
        c_ref[...] = jnp.zeros_like(c_ref)

    cv = c_ref[...]
    mu = jnp.mean(cv, axis=-1, keepdims=True)
    cc = cv - mu
    var = jnp.mean(cc * cc, axis=-1, keepdims=True)
    z = cc * lax.rsqrt(var + EPS) * lng_ref[...] + lnb_ref[...]
    z = jax.nn.silu(z).astype(bf16)
    z = _split_k_dot(qm_ref, z).astype(bf16)
    mix = jnp.dot(z, w2_ref[...], preferred_element_type=f32) + b2_ref[...]
    o_ref[...] = xr_ref[...] + mix

    rho = lax.broadcasted_iota(jnp.int32, (ext, 1), 0)
    e = (rho % SUBLANES) * p + rho // SUBLANES
    inside = ((e >= HALO) | (i > 0)) & ((e < HALO + tm) | (i < n - 1))
    for cb in range(n_cb):
        cs = slice(cb * MXU_COLS, (cb + 1) * MXU_COLS)
        u = jnp.where(inside, y_ref[slot, cb] * jax.nn.sigmoid(y_ref[slot, n_cb + cb]), 0.0)
        u_ref[pad:pad + ext, cs] = u
        tail = u[ext - pad:].reshape(CONV_PAD, SUBLANES, MXU_COLS)
        head = u[:pad].reshape(CONV_PAD, SUBLANES, MXU_COLS)
        u_ref[:pad, cs] = pltpu.roll(tail, 1, axis=1).reshape(pad, MXU_COLS)
        u_ref[pad + ext:, cs] = pltpu.roll(head, SUBLANES - 1, axis=1).reshape(pad, MXU_COLS)

    interleaved_input(xm_ref, xp_ref[...], xn_ref[...])

    def conv_group(gi):
        base = pl.multiple_of(gi * (CONV_GROUP * SUBLANES), CONV_GROUP * SUBLANES)
        for c in range(d // CONV_LANES):
            cs = slice(c * CONV_LANES, (c + 1) * CONV_LANES)
            taps = {}
            acc = [None] * CONV_GROUP
            for m in range(CONV_GROUP + CONV_WIDTH - 1):
                rows = u_ref[pl.ds(base + m * SUBLANES, SUBLANES), cs]
                for q in range(CONV_GROUP):
                    k = m - q
                    if 0 <= k < CONV_WIDTH:
                        if k not in taps:
                            taps[k] = wdw_ref[k * SUBLANES:(k + 1) * SUBLANES, cs]
                        term = rows * taps[k]
                        acc[q] = term if acc[q] is None else acc[q] + term
            for q in range(CONV_GROUP):
                c_ref[pl.ds(base + q * SUBLANES, SUBLANES), cs] = acc[q] + bdw_ref[:, cs]

    def body(t, carry):
        conv_group(jnp.minimum(2 * t, groups - 1))
        conv_group(jnp.minimum(2 * t + 1, groups - 1))
        pointwise_block(1 - slot, jnp.minimum(t, 2 * n_cb - 1))
        return carry

    @pl.when(i < n)
    def _():
        lax.fori_loop(0, max(-(-groups // 2), 2 * n_cb), body, 0)


def _conv_mixer(x, g, w1, b1, wdw, bdw, lng, lnb, w2, b2):
    b, s, d = x.shape
    tm = min(TOKEN_TILE, s)
    n = s // tm
    nh = tm // HALO
    last = s // HALO - 1
    ext = tm + 2 * HALO
    n_blk = 2 * d // MXU_COLS
    assert ext % SUBLANES == 0 and (ext // SUBLANES) % CONV_GROUP == 0
    assert ext // SUBLANES >= CONV_PAD and HALO >= CONV_PAD and n >= 2
    pm, qm = _interleave_matrices(tm)
    wdw8 = jnp.repeat(wdw, SUBLANES, axis=0)
    w1r = w1.reshape(d, n_blk, MXU_COLS).transpose(1, 0, 2)
    b1r = b1.reshape(n_blk, 1, MXU_COLS)
    nxt = lambda i: jnp.minimum(i + 1, n - 1)
    main = pl.BlockSpec((None, tm, d), lambda bi, i: (bi, jnp.minimum(i, n - 1), 0))
    done = pl.BlockSpec((None, tm, d), lambda bi, i: (bi, jnp.maximum(i - 1, 0), 0))
    main_next = pl.BlockSpec((None, tm, d), lambda bi, i: (bi, nxt(i), 0))
    prev_next = pl.BlockSpec((None, HALO, d), lambda bi, i: (bi, nxt(i) * nh - 1, 0))
    next_next = pl.BlockSpec((None, HALO, d),
                             lambda bi, i: (bi, jnp.minimum((nxt(i) + 1) * nh, last), 0))
    vec = lambda a: a.reshape(1, -1)
    return pl.pallas_call(
        functools.partial(_conv_kernel, tm=tm, d=d),
        out_shape=jax.ShapeDtypeStruct((b, s, d), f32),
        grid=(b, n + 1),
        in_specs=[main, done, main_next, prev_next, next_next, _resident((1, d)), _resident(pm.shape),
                  _resident(qm.shape), _resident(w1r.shape), _resident(b1r.shape),
                  _resident(wdw8.shape), _resident((1, d)), _resident((1, d)), _resident((1, d)),
                  _resident(w2.shape), _resident((1, d))],
        out_specs=done,
        scratch_shapes=[pltpu.VMEM((ext, d), bf16),
                        pltpu.VMEM((2, n_blk, ext, MXU_COLS), f32),
                        pltpu.VMEM((ext + 2 * CONV_PAD * SUBLANES, d), f32),
                        pltpu.VMEM((ext, d), f32)],
        compiler_params=_params(2),
        name="conv_mixer",
    )(x, x, x, x, x, vec(g), pm, qm, w1r, b1r, wdw8, vec(bdw), vec(lng), vec(lnb), w2, vec(b2))


def _hproj_kernel(x_ref, g_ref, w_ref, lbf_ref, lbb_ref,
                  q_ref, v_ref, lff_ref, lfb_ref, sg_ref, *, d):
    h = _rms(x_ref[...], g_ref[...]).astype(bf16)

    for c in range(d // MXU_COLS):
        cs = slice(c * MXU_COLS, (c + 1) * MXU_COLS)

        def proj(j):
            return jnp.dot(h, w_ref[:, j * d + c * MXU_COLS:j * d + (c + 1) * MXU_COLS],
                           preferred_element_type=f32)

        q_ref[:, cs] = jax.nn.silu(proj(0))
        v_ref[:, cs] = proj(1).astype(bf16)
        for j, lb_ref, lf_ref in ((2, lbf_ref, lff_ref), (3, lbb_ref, lfb_ref)):
            lb = lb_ref[:, cs]
            lf_ref[:, cs] = jnp.log(lb + (1.0 - lb) * jax.nn.sigmoid(proj(j)))
        sg_ref[:, cs] = jax.nn.silu(proj(4))


def _hgrn_proj(x, g, w_in, lb_f, lb_b):
    t, d = x.shape
    tm = min(TOKEN_TILE, t)
    row = pl.BlockSpec((tm, d), lambda i: (i, 0))
    sd = lambda dt: jax.ShapeDtypeStruct((t, d), dt)
    return pl.pallas_call(
        functools.partial(_hproj_kernel, d=d),
        out_shape=(sd(f32), sd(bf16), sd(f32), sd(f32), sd(f32)),
        grid=(t // tm,),
        in_specs=[row, _resident((1, d)), _resident(w_in.shape), _resident((1, d)),
                  _resident((1, d))],
        out_specs=(row,) * 5,
        compiler_params=_params(1),
        name="hgrn_proj",
    )(x, g.reshape(1, d), w_in, lb_f.reshape(1, d), lb_b.reshape(1, d))


def _chunk_cumsum(lf_ref, b_ref, k_ref, chunk, reverse):
    n, d = lf_ref.shape
    sub = lax.broadcasted_iota(jnp.int32, (SUBLANES, 1), 0)
    peak = jnp.zeros((SUBLANES, SCAN_LANES), f32)
    for c in range(n // chunk):
        for l in range(d // SCAN_LANES):
            ls = slice(l * SCAN_LANES, (l + 1) * SCAN_LANES)
            tiles = range(chunk // SUBLANES)
            carry = None
            for g in (reversed(tiles) if reverse else tiles):
                rs = slice(c * chunk + g * SUBLANES, c * chunk + (g + 1) * SUBLANES)
                t = lf_ref[rs, ls]
                k_ref[rs, ls] = 1.0 - jnp.exp(t)
                shift = 1
                while shift < SUBLANES:
                    if reverse:
                        moved = pltpu.roll(t, SUBLANES - shift, axis=0)
                        keep = sub < SUBLANES - shift
                    else:
                        moved = pltpu.roll(t, shift, axis=0)
                        keep = sub >= shift
                    t = t + jnp.where(keep, moved, 0.0)
                    shift *= 2
                if carry is not None:
                    t = t + carry
                b_ref[rs, ls] = t
                peak = jnp.maximum(peak, jnp.abs(t))
                end = t[:1] if reverse else t[SUBLANES - 1:]
                carry = jnp.broadcast_to(end, (SUBLANES, SCAN_LANES))
    return jnp.max(peak, keepdims=True)


def _scan_factorised(q_ref, v_ref, k_ref, b_ref, o_ref, st_ref, *, reverse, chunk, heads):
    n = q_ref.shape[0]
    ti = lax.broadcasted_iota(jnp.int32, (chunk, chunk), 0)
    si = lax.broadcasted_iota(jnp.int32, (chunk, chunk), 1)
    causal = (si >= ti) if reverse else (si <= ti)
    order = range(n // chunk - 1, -1, -1) if reverse else range(n // chunk)
    for c in order:
        rs = slice(c * chunk, (c + 1) * chunk)
        b = b_ref[rs, :]
        b_end = b[:1] if reverse else b[chunk - 1:]
        mid = 0.5 * b_end
        e_mid = jnp.exp(mid)
        e_end = jnp.exp(b_end)
        qd = q_ref[rs, :] * jnp.exp(b - mid)
        kd = k_ref[rs, :] * jnp.exp(mid - b)
        qs = (qd * e_mid).astype(bf16)
        ks = (kd * e_mid).astype(bf16)
        qd = qd.astype(bf16)
        kd = kd.astype(bf16)
        for h in range(heads):
            hs = slice(h * HEAD_DIM, (h + 1) * HEAD_DIM)
            v = v_ref[rs, hs]
            sc = lax.dot_general(qd[:, hs], kd[:, hs], (((1,), (1,)), ((), ())),
                                 preferred_element_type=f32)
            sc = jnp.where(causal, sc, 0.0).astype(bf16)
            st = st_ref[h]
            o_ref[rs, hs] = (jnp.dot(sc, v, preferred_element_type=f32)
                             + lax.dot_general(qs[:, hs], st.astype(bf16),
                                               (((1,), (1,)), ((), ())),
                                               preferred_element_type=f32))
            st_ref[h] = st * e_end[:, hs] + lax.dot_general(
                v, ks[:, hs], (((0,), (0,)), ((), ())), preferred_element_type=f32)


def _scan_direct(q_ref, v_ref, k_ref, b_ref, o_ref, st_ref, *, reverse, chunk, heads):
    n = q_ref.shape[0]
    nc = n // chunk
    pos = lax.broadcasted_iota(jnp.int32, (chunk, 1), 0)

    def chunk_body(ci, carry):
        c = nc - 1 - ci if reverse else ci
        r0 = pl.multiple_of(c * chunk, chunk)
        rs = pl.ds(r0, chunk)

        def row_body(t, carry):
            b = b_ref[rs, :]
            diff = b_ref[pl.ds(r0 + t, 1), :] - b
            seen = (pos >= t) if reverse else (pos <= t)
            p = (q_ref[pl.ds(r0 + t, 1), :] * k_ref[rs, :]
                 * jnp.exp(jnp.where(seen, diff, -jnp.inf)))
            outs = []
            for h in range(heads):
                hs = slice(h * HEAD_DIM, (h + 1) * HEAD_DIM)
                sc = jnp.sum(p[:, hs], axis=-1, keepdims=True)
                outs.append(jnp.sum(sc * v_ref[rs, hs].astype(f32), axis=0, keepdims=True))
            o_ref[pl.ds(r0 + t, 1), :] = jnp.concatenate(outs, axis=1)
            return carry

        lax.fori_loop(0, chunk, row_body, 0)
        b = b_ref[rs, :]
        b_end = b[:1] if reverse else b[chunk - 1:]
        e_end = jnp.exp(b_end)
        qs = (q_ref[rs, :] * jnp.exp(b)).astype(bf16)
        ks = (k_ref[rs, :] * jnp.exp(b_end - b)).astype(bf16)
        for h in range(heads):
            hs = slice(h * HEAD_DIM, (h + 1) * HEAD_DIM)
            st = st_ref[h]
            o_ref[rs, hs] += lax.dot_general(qs[:, hs], st.astype(bf16),
                                             (((1,), (1,)), ((), ())),
                                             preferred_element_type=f32)
            st_ref[h] = st * e_end[:, hs] + lax.dot_general(
                v_ref[rs, hs], ks[:, hs], (((0,), (0,)), ((), ())),
                preferred_element_type=f32)
        return carry

    lax.fori_loop(0, nc, chunk_body, 0)


def _scan_kernel(qf_ref, vf_ref, lff_ref, lff0_ref, qb_ref, vb_ref, lfb_ref, lfb0_ref,
                 of_ref, ob_ref, stf_ref, stb_ref, bf_ref, bb_ref, kf_ref, kb_ref, tame_ref,
                 *, chunk, heads):
    i = pl.program_id(1)
    slot = i % 2
    nxt = 1 - slot

    def prepare(lf_f, lf_b, dst):
        decay = jnp.maximum(
            _chunk_cumsum(lf_f, bf_ref.at[dst], kf_ref.at[dst], chunk, reverse=False),
            _chunk_cumsum(lf_b, bb_ref.at[dst], kb_ref.at[dst], chunk, reverse=True))[0, 0]
        tame_ref[dst] = (decay <= MAX_CHUNK_DECAY).astype(jnp.int32)

    @pl.when(i == 0)
    def _():
        stf_ref[...] = jnp.zeros_like(stf_ref)
        stb_ref[...] = jnp.zeros_like(stb_ref)
        prepare(lff0_ref, lfb0_ref, slot)

    fwd = (qf_ref, vf_ref, kf_ref.at[slot], bf_ref.at[slot], of_ref, stf_ref)
    bwd = (qb_ref, vb_ref, kb_ref.at[slot], bb_ref.at[slot], ob_ref, stb_ref)
    tame = tame_ref[slot] == 1

    @pl.when(tame)
    def _():
        _scan_factorised(*fwd, reverse=False, chunk=chunk, heads=heads)
        _scan_factorised(*bwd, reverse=True, chunk=chunk, heads=heads)
        prepare(lff_ref, lfb_ref, nxt)

    @pl.when(jnp.logical_not(tame))
    def _():
        _scan_direct(*fwd, reverse=False, chunk=chunk, heads=heads)
        _scan_direct(*bwd, reverse=True, chunk=chunk, heads=heads)
        prepare(lff_ref, lfb_ref, nxt)


def _hgrn_scan(q, v, lff, lfb):
    b, s, d = q.shape
    blk = min(SCAN_BLOCK, s)
    chunk = min(SCAN_CHUNK, blk)
    nb = s // blk
    heads = d // HEAD_DIM
    fwd = pl.BlockSpec((None, blk, d), lambda bi, i: (bi, i, 0))
    bwd = pl.BlockSpec((None, blk, d), lambda bi, i: (bi, nb - 1 - i, 0))
    fwd_next = pl.BlockSpec((None, blk, d), lambda bi, i: (bi, jnp.minimum(i + 1, nb - 1), 0))
    bwd_next = pl.BlockSpec((None, blk, d), lambda bi, i: (bi, jnp.maximum(nb - 2 - i, 0), 0))
    fwd_first = pl.BlockSpec((None, blk, d), lambda bi, i: (bi, 0, 0))
    bwd_first = pl.BlockSpec((None, blk, d), lambda bi, i: (bi, nb - 1, 0))
    out = jax.ShapeDtypeStruct((b, s, d), f32)
    state = pltpu.VMEM((heads, HEAD_DIM, HEAD_DIM), f32)
    slots = pltpu.VMEM((2, blk, d), f32)
    return pl.pallas_call(
        functools.partial(_scan_kernel, chunk=chunk, heads=heads),
        out_shape=(out, out),
        grid=(b, nb),
        in_specs=[fwd, fwd, fwd_next, fwd_first, bwd, bwd, bwd_next, bwd_first],
        out_specs=(fwd, bwd),
        scratch_shapes=[state, state, slots, slots, slots, slots, pltpu.SMEM((2,), jnp.int32)],
        compiler_params=_params(2),
        name="hgrn_scan",
    )(q, v, lff, lff, q, v, lfb, lfb)


def kernel(x, norm_g, ffn_w13, ffn_w2, conv_w_pw1, conv_b_pw1, conv_w_dw, conv_b_dw,
           conv_ln_g, conv_ln_b, conv_w_pw2, conv_b_pw2, hgrn_w_in, hgrn_lb,
           hgrn_gn_g, hgrn_w_out, final_g):
    b, s, d = x.shape
    depth = norm_g.shape[0]
    t = b * s
    lb_table = jnp.cumsum(jax.nn.softmax(hgrn_lb.astype(f32), axis=1), axis=1)
    lb_table = lb_table - lb_table[:, :1]
    wb = lambda w: w.astype(bf16)

    x = x.reshape(t, d)
    for layer in range(depth):
        x = _ffn(x, norm_g[layer, 0], wb(ffn_w13[layer, 0]), wb(ffn_w2[layer, 0]),
                 final_g, final=False)
        j = layer // 2
        hgrn_out = None
        if layer % 2 == 0:
            x = _conv_mixer(x.reshape(b, s, d), norm_g[layer, 1], wb(conv_w_pw1[j]),
                            conv_b_pw1[j], conv_w_dw[j], conv_b_dw[j], conv_ln_g[j],
                            conv_ln_b[j], wb(conv_w_pw2[j]), conv_b_pw2[j]).reshape(t, d)
        else:
            q, v, lff, lfb, sg = _hgrn_proj(
                x, norm_g[layer, 1], wb(hgrn_w_in[j]), lb_table[0, layer], lb_table[1, layer])
            r3 = lambda a: a.reshape(b, s, d)
            o_f, o_b = _hgrn_scan(r3(q), r3(v), r3(lff), r3(lfb))
            hgrn_out = (o_f.reshape(t, d), o_b.reshape(t, d), sg, hgrn_gn_g[j],
                        wb(hgrn_w_out[j]))
        x = _ffn(x, norm_g[layer, 2], wb(ffn_w13[layer, 1]), wb(ffn_w2[layer, 1]),
                 final_g, final=(layer == depth - 1), hgrn_out=hgrn_out)
    return x.reshape(b, s, d)
```

```python
import functools

import jax
import jax.numpy as jnp
import numpy as np
from jax import lax
from jax.experimental import pallas as pl
from jax.experimental.pallas import tpu as pltpu

EPS = 1e-6
FFN_RES = 0.5
CONV_WIDTH = 31
CONV_PAD = (CONV_WIDTH - 1) // 2
HEAD_DIM = 128
N_PROJ = 5

SUBLANES = 8
Q_TILE = 2 * SUBLANES
LANES = 128
MXU_COLS = 256
HALO = 2 * SUBLANES
VMEM_LIMIT = 56 * 1024 * 1024

TOKEN_TILE = 512
CAST_STEPS = 16
SCAN_BLOCK = 512
SCAN_CHUNK = 128
SCAN_LANES = 256
CONV_GROUP = 4
CONV_LANES = 512
MAX_CHUNK_DECAY = 80.0

f32 = jnp.float32
bf16 = jnp.bfloat16


def _rms(x, g):
    return x * lax.rsqrt(jnp.mean(x * x, axis=-1, keepdims=True) + EPS) * g


def _resident(shape):
    return pl.BlockSpec(shape, lambda *_: (0,) * len(shape), pipeline_mode=pl.Buffered(1))


def _weight_chunks(w):
    rows = w.shape[0] // CAST_STEPS
    assert rows * CAST_STEPS == w.shape[0] and rows % Q_TILE == 0
    return pl.BlockSpec((rows, w.shape[1]), lambda s: (jnp.minimum(s, CAST_STEPS - 1), 0))


def _cast_chunk(chunk_ref, w_ref, step):
    rows = chunk_ref.shape[0]
    w_ref[pl.ds(pl.multiple_of(step * rows, rows), rows), :] = chunk_ref[...].astype(bf16)


def _params(n_axes):
    return pltpu.CompilerParams(dimension_semantics=("arbitrary",) * n_axes,
                                vmem_limit_bytes=VMEM_LIMIT)


def _ffn_kernel(*refs, d_ff, fc, final, heads):
    if heads:
        (x_ref, of_ref, ob_ref, sg_ref, gn_ref, woc_ref, g_ref, w13c_ref, w2c_ref, fg_ref,
         o_ref, a_ref, w13_ref, w2_ref, y_ref, wo_ref) = refs
    else:
        x_ref, g_ref, w13c_ref, w2c_ref, fg_ref, o_ref, a_ref, w13_ref, w2_ref = refs
    step = pl.program_id(0)

    @pl.when(step < CAST_STEPS)
    def _():
        _cast_chunk(w13c_ref, w13_ref, step)
        _cast_chunk(w2c_ref, w2_ref, step)
        if heads:
            _cast_chunk(woc_ref, wo_ref, step)

    @pl.when(step >= CAST_STEPS)
    def _():
        if heads:
            for h in range(heads):
                hs = slice(h * HEAD_DIM, (h + 1) * HEAD_DIM)
                o = of_ref[:, hs] + ob_ref[:, hs]
                o = o * lax.rsqrt(jnp.mean(o * o, axis=-1, keepdims=True) + EPS)
                y_ref[:, hs] = (o * gn_ref[:, hs] * sg_ref[:, hs]).astype(bf16)
            x = x_ref[...] + jnp.dot(y_ref[...], wo_ref[...], preferred_element_type=f32)
        else:
            x = x_ref[...]
        h = _rms(x, g_ref[...]).astype(bf16)
        for j in range(d_ff // fc):
            gate = jnp.dot(h, w13_ref[:, j * fc:(j + 1) * fc], preferred_element_type=f32)
            up = jnp.dot(h, w13_ref[:, d_ff + j * fc:d_ff + (j + 1) * fc],
                         preferred_element_type=f32)
            a_ref[:, j * fc:(j + 1) * fc] = (jax.nn.silu(gate) * up).astype(bf16)
        y = jnp.dot(a_ref[...], w2_ref[...], preferred_element_type=f32)
        out = x + FFN_RES * y
        if final:
            out = _rms(out, fg_ref[...])
        o_ref[...] = out


def _ffn(x, g, w13, w2, final_g, *, final, hgrn_out=None):
    t, d = x.shape
    d_ff = w2.shape[0]
    tm = min(TOKEN_TILE, t)
    fc = MXU_COLS if d_ff % MXU_COLS == 0 else d_ff
    row = pl.BlockSpec((tm, d), lambda s: (jnp.maximum(s - CAST_STEPS, 0), 0))
    args = [x]
    specs = [row]
    scratch = [pltpu.VMEM((tm, d_ff), bf16), pltpu.VMEM(w13.shape, bf16), pltpu.VMEM(w2.shape, bf16)]
    if hgrn_out is not None:
        o_f, o_b, sg, gn, w_out = hgrn_out
        args += [o_f, o_b, sg, gn.reshape(1, d), w_out]
        specs += [row, row, row, _resident((1, d)), _weight_chunks(w_out)]
        scratch += [pltpu.VMEM((tm, d), bf16), pltpu.VMEM(w_out.shape, bf16)]
    args += [g.reshape(1, d), w13, w2, final_g.reshape(1, d)]
    specs += [_resident((1, d)), _weight_chunks(w13), _weight_chunks(w2), _resident((1, d))]
    return pl.pallas_call(
        functools.partial(_ffn_kernel, d_ff=d_ff, fc=fc, final=final,
                          heads=0 if hgrn_out is None else d // HEAD_DIM),
        out_shape=jax.ShapeDtypeStruct((t, d), f32),
        grid=(CAST_STEPS + t // tm,),
        in_specs=specs,
        out_specs=row,
        scratch_shapes=scratch,
        compiler_params=_params(1),
        name="ffn",
    )(*args)


def _interleave_matrices(tm):
    ext = tm + 2 * HALO
    p = ext // SUBLANES
    rho = np.arange(ext)
    to_il = np.zeros((ext, ext), np.float32)
    to_il[rho, (rho % SUBLANES) * p + rho // SUBLANES] = 1.0
    e = np.arange(tm) + HALO
    to_nat = np.zeros((tm, ext), np.float32)
    to_nat[np.arange(tm), (e % p) * SUBLANES + e // p] = 1.0
    return jnp.asarray(to_il, bf16), jnp.asarray(to_nat, bf16)


def _conv_kernel(x_ref, xp_ref, xn_ref, g_ref, pm_ref, qm_ref, w1_ref, b1_ref, wdw_ref, bdw_ref,
                 lng_ref, lnb_ref, w2_ref, b2_ref, o_ref, hp_ref, u_ref, c_ref, *, tm, d):
    i = pl.program_id(1)
    n = pl.num_programs(1)
    ext = tm + 2 * HALO
    p = ext // SUBLANES
    pad = CONV_PAD * SUBLANES
    xe = jnp.concatenate([xp_ref[...], x_ref[...], xn_ref[...]], axis=0)
    h = _rms(xe, g_ref[...]).astype(bf16)
    hp_ref[...] = jnp.dot(pm_ref[...], h, preferred_element_type=f32).astype(bf16)

    rho = lax.broadcasted_iota(jnp.int32, (ext, 1), 0)
    e = (rho % SUBLANES) * p + rho // SUBLANES
    inside = ((e >= HALO) | (i > 0)) & ((e < HALO + tm) | (i < n - 1))

    for cb in range(d // MXU_COLS):
        cs = slice(cb * MXU_COLS, (cb + 1) * MXU_COLS)
        gs = slice(d + cb * MXU_COLS, d + (cb + 1) * MXU_COLS)
        a = jnp.dot(hp_ref[...], w1_ref[:, cs], preferred_element_type=f32) + b1_ref[:, cs]
        gate = jnp.dot(hp_ref[...], w1_ref[:, gs], preferred_element_type=f32) + b1_ref[:, gs]
        u = jnp.where(inside, a * jax.nn.sigmoid(gate), 0.0)
        u_ref[pad:pad + ext, cs] = u
        tail = u[ext - pad:].reshape(CONV_PAD, SUBLANES, MXU_COLS)
        head = u[:pad].reshape(CONV_PAD, SUBLANES, MXU_COLS)
        u_ref[:pad, cs] = pltpu.roll(tail, 1, axis=1).reshape(pad, MXU_COLS)
        u_ref[pad + ext:, cs] = pltpu.roll(head, SUBLANES - 1, axis=1).reshape(pad, MXU_COLS)

    def group(gi, carry):
        base = pl.multiple_of(gi * (CONV_GROUP * SUBLANES), CONV_GROUP * SUBLANES)
        for c in range(d // CONV_LANES):
            cs = slice(c * CONV_LANES, (c + 1) * CONV_LANES)
            taps = {}
            acc = [None] * CONV_GROUP
            for m in range(CONV_GROUP + CONV_WIDTH - 1):
                rows = u_ref[pl.ds(base + m * SUBLANES, SUBLANES), cs]
                for q in range(CONV_GROUP):
                    k = m - q
                    if 0 <= k < CONV_WIDTH:
                        if k not in taps:
                            taps[k] = wdw_ref[k * SUBLANES:(k + 1) * SUBLANES, cs]
                        term = rows * taps[k]
                        acc[q] = term if acc[q] is None else acc[q] + term
            for q in range(CONV_GROUP):
                c_ref[pl.ds(base + q * SUBLANES, SUBLANES), cs] = acc[q] + bdw_ref[:, cs]
        return carry

    lax.fori_loop(0, p // CONV_GROUP, group, 0)

    cv = c_ref[...]
    mu = jnp.mean(cv, axis=-1, keepdims=True)
    cc = cv - mu
    var = jnp.mean(cc * cc, axis=-1, keepdims=True)
    z = cc * lax.rsqrt(var + EPS) * lng_ref[...] + lnb_ref[...]
    z = jax.nn.silu(z).astype(bf16)
    z = jnp.dot(qm_ref[...], z, preferred_element_type=f32).astype(bf16)
    mix = jnp.dot(z, w2_ref[...], preferred_element_type=f32) + b2_ref[...]
    o_ref[...] = x_ref[...] + mix


def _conv_mixer(x, g, w1, b1, wdw, bdw, lng, lnb, w2, b2):
    b, s, d = x.shape
    tm = min(TOKEN_TILE, s)
    nh = tm // HALO
    last = s // HALO - 1
    ext = tm + 2 * HALO
    assert ext % SUBLANES == 0 and (ext // SUBLANES) % CONV_GROUP == 0
    assert ext // SUBLANES >= CONV_PAD and HALO >= CONV_PAD
    pm, qm = _interleave_matrices(tm)
    wdw8 = jnp.repeat(wdw, SUBLANES, axis=0)
    main = pl.BlockSpec((None, tm, d), lambda bi, i: (bi, i, 0))
    prev = pl.BlockSpec((None, HALO, d), lambda bi, i: (bi, jnp.maximum(i * nh - 1, 0), 0))
    nxt = pl.BlockSpec((None, HALO, d), lambda bi, i: (bi, jnp.minimum((i + 1) * nh, last), 0))
    vec = lambda a: a.reshape(1, -1)
    return pl.pallas_call(
        functools.partial(_conv_kernel, tm=tm, d=d),
        out_shape=jax.ShapeDtypeStruct((b, s, d), f32),
        grid=(b, s // tm),
        in_specs=[main, prev, nxt, _resident((1, d)), _resident(pm.shape), _resident(qm.shape),
                  _resident(w1.shape), _resident((1, 2 * d)), _resident(wdw8.shape),
                  _resident((1, d)), _resident((1, d)), _resident((1, d)), _resident(w2.shape),
                  _resident((1, d))],
        out_specs=main,
        scratch_shapes=[pltpu.VMEM((ext, d), bf16),
                        pltpu.VMEM((ext + 2 * CONV_PAD * SUBLANES, d), f32),
                        pltpu.VMEM((ext, d), f32)],
        compiler_params=_params(2),
        name="conv_mixer",
    )(x, x, x, vec(g), pm, qm, w1, vec(b1), wdw8, vec(bdw), vec(lng), vec(lnb), w2, vec(b2))


def _hproj_kernel(x_ref, g_ref, wc_ref, lbf_ref, lbb_ref,
                  q_ref, v_ref, lff_ref, lfb_ref, sg_ref, w_ref, *, d):
    step = pl.program_id(0)

    @pl.when(step < CAST_STEPS)
    def _():
        _cast_chunk(wc_ref, w_ref, step)

    @pl.when(step >= CAST_STEPS)
    def _():
        h = _rms(x_ref[...], g_ref[...]).astype(bf16)

        for c in range(d // MXU_COLS):
            cs = slice(c * MXU_COLS, (c + 1) * MXU_COLS)

            def proj(j):
                return jnp.dot(h, w_ref[:, j * d + c * MXU_COLS:j * d + (c + 1) * MXU_COLS],
                               preferred_element_type=f32)

            q_ref[:, cs] = jax.nn.silu(proj(0)).astype(bf16)
            v_ref[:, cs] = proj(1).astype(bf16)
            for j, lb_ref, lf_ref in ((2, lbf_ref, lff_ref), (3, lbb_ref, lfb_ref)):
                lb = lb_ref[:, cs]
                lf_ref[:, cs] = jnp.log(lb + (1.0 - lb) * jax.nn.sigmoid(proj(j)))
            sg_ref[:, cs] = jax.nn.silu(proj(4))


def _hgrn_proj(x, g, w_in, lb_f, lb_b):
    t, d = x.shape
    tm = min(TOKEN_TILE, t)
    row = pl.BlockSpec((tm, d), lambda s: (jnp.maximum(s - CAST_STEPS, 0), 0))
    sd = lambda dt: jax.ShapeDtypeStruct((t, d), dt)
    return pl.pallas_call(
        functools.partial(_hproj_kernel, d=d),
        out_shape=(sd(bf16), sd(bf16), sd(f32), sd(f32), sd(f32)),
        grid=(CAST_STEPS + t // tm,),
        in_specs=[row, _resident((1, d)), _weight_chunks(w_in), _resident((1, d)),
                  _resident((1, d))],
        out_specs=(row,) * 5,
        scratch_shapes=[pltpu.VMEM(w_in.shape, bf16)],
        compiler_params=_params(1),
        name="hgrn_proj",
    )(x, g.reshape(1, d), w_in, lb_f.reshape(1, d), lb_b.reshape(1, d))


def _chunk_cumsum(lf_ref, b_ref, k_ref, chunk, reverse):
    n, d = lf_ref.shape
    sub = lax.broadcasted_iota(jnp.int32, (SUBLANES, 1), 0)
    peak = jnp.zeros((SUBLANES, SCAN_LANES), f32)
    for c in range(n // chunk):
        for l in range(d // SCAN_LANES):
            ls = slice(l * SCAN_LANES, (l + 1) * SCAN_LANES)
            tiles = range(chunk // SUBLANES)
            carry = None
            for g in (reversed(tiles) if reverse else tiles):
                rs = slice(c * chunk + g * SUBLANES, c * chunk + (g + 1) * SUBLANES)
                t = lf_ref[rs, ls]
                k_ref[rs, ls] = 1.0 - jnp.exp(t)
                shift = 1
                while shift < SUBLANES:
                    if reverse:
                        moved = pltpu.roll(t, SUBLANES - shift, axis=0)
                        keep = sub < SUBLANES - shift
                    else:
                        moved = pltpu.roll(t, shift, axis=0)
                        keep = sub >= shift
                    t = t + jnp.where(keep, moved, 0.0)
                    shift *= 2
                if carry is not None:
                    t = t + carry
                b_ref[rs, ls] = t
                peak = jnp.maximum(peak, jnp.abs(t))
                end = t[:1] if reverse else t[SUBLANES - 1:]
                carry = jnp.broadcast_to(end, (SUBLANES, SCAN_LANES))
    return jnp.max(peak, keepdims=True)


def _scan_factorised(q_ref, v_ref, k_ref, b_ref, o_ref, st_ref, *, reverse, chunk, heads):
    n = q_ref.shape[0]
    ti = lax.broadcasted_iota(jnp.int32, (chunk, chunk), 0)
    si = lax.broadcasted_iota(jnp.int32, (chunk, chunk), 1)
    causal = (si >= ti) if reverse else (si <= ti)
    order = range(n // chunk - 1, -1, -1) if reverse else range(n // chunk)
    for c in order:
        rs = slice(c * chunk, (c + 1) * chunk)
        b = b_ref[rs, :]
        b_end = b[:1] if reverse else b[chunk - 1:]
        mid = 0.5 * b_end
        e_mid = jnp.exp(mid)
        e_end = jnp.exp(b_end)
        qd = q_ref[rs, :] * jnp.exp(b - mid)
        kd = k_ref[rs, :] * jnp.exp(mid - b)
        qs = (qd * e_mid).astype(bf16)
        ks = (kd * e_mid).astype(bf16)
        qd = qd.astype(bf16)
        kd = kd.astype(bf16)
        for h in range(heads):
            hs = slice(h * HEAD_DIM, (h + 1) * HEAD_DIM)
            v = v_ref[rs, hs]
            sc = lax.dot_general(qd[:, hs], kd[:, hs], (((1,), (1,)), ((), ())),
                                 preferred_element_type=f32)
            sc = jnp.where(causal, sc, 0.0).astype(bf16)
            st = st_ref[h]
            o_ref[rs, hs] = (jnp.dot(sc, v, preferred_element_type=f32)
                             + lax.dot_general(qs[:, hs], st.astype(bf16),
                                               (((1,), (1,)), ((), ())),
                                               preferred_element_type=f32))
            st_ref[h] = st * e_end[:, hs] + lax.dot_general(
                v, ks[:, hs], (((0,), (0,)), ((), ())), preferred_element_type=f32)


def _scan_direct(q_ref, v_ref, k_ref, b_ref, o_ref, st_ref, *, reverse, chunk, heads):
    n = q_ref.shape[0]
    nc = n // chunk
    pos = lax.broadcasted_iota(jnp.int32, (chunk, 1), 0)
    tile_pos = lax.broadcasted_iota(jnp.int32, (Q_TILE, 1), 0)

    def chunk_body(ci, carry):
        c = nc - 1 - ci if reverse else ci
        r0 = pl.multiple_of(c * chunk, chunk)
        rs = pl.ds(r0, chunk)

        def row_body(t, carry):
            b = b_ref[rs, :]
            diff = b_ref[pl.ds(r0 + t, 1), :] - b
            seen = (pos >= t) if reverse else (pos <= t)
            t0 = pl.multiple_of((t // Q_TILE) * Q_TILE, Q_TILE)
            q_tile = q_ref[pl.ds(r0 + t0, Q_TILE), :].astype(f32)
            q_row = jnp.sum(jnp.where(tile_pos == t - t0, q_tile, 0.0), axis=0, keepdims=True)
            p = q_row * k_ref[rs, :] * jnp.exp(jnp.where(seen, diff, -jnp.inf))
            outs = []
            for h in range(heads):
                hs = slice(h * HEAD_DIM, (h + 1) * HEAD_DIM)
                sc = jnp.sum(p[:, hs], axis=-1, keepdims=True)
                outs.append(jnp.sum(sc * v_ref[rs, hs].astype(f32), axis=0, keepdims=True))
            o_ref[pl.ds(r0 + t, 1), :] = jnp.concatenate(outs, axis=1)
            return carry

        lax.fori_loop(0, chunk, row_body, 0)
        b = b_ref[rs, :]
        b_end = b[:1] if reverse else b[chunk - 1:]
        e_end = jnp.exp(b_end)
        qs = (q_ref[rs, :] * jnp.exp(b)).astype(bf16)
        ks = (k_ref[rs, :] * jnp.exp(b_end - b)).astype(bf16)
        for h in range(heads):
            hs = slice(h * HEAD_DIM, (h + 1) * HEAD_DIM)
            st = st_ref[h]
            o_ref[rs, hs] += lax.dot_general(qs[:, hs], st.astype(bf16),
                                             (((1,), (1,)), ((), ())),
                                             preferred_element_type=f32)
            st_ref[h] = st * e_end[:, hs] + lax.dot_general(
                v_ref[rs, hs], ks[:, hs], (((0,), (0,)), ((), ())),
                preferred_element_type=f32)
        return carry

    lax.fori_loop(0, nc, chunk_body, 0)


def _scan_kernel(qf_ref, vf_ref, lff_ref, lff0_ref, qb_ref, vb_ref, lfb_ref, lfb0_ref,
                 of_ref, ob_ref, stf_ref, stb_ref, bf_ref, bb_ref, kf_ref, kb_ref, tame_ref,
                 *, chunk, heads):
    i = pl.program_id(1)
    slot = i % 2
    nxt = 1 - slot

    def prepare(lf_f, lf_b, dst):
        decay = jnp.maximum(
            _chunk_cumsum(lf_f, bf_ref.at[dst], kf_ref.at[dst], chunk, reverse=False),
            _chunk_cumsum(lf_b, bb_ref.at[dst], kb_ref.at[dst], chunk, reverse=True))[0, 0]
        tame_ref[dst] = (decay <= MAX_CHUNK_DECAY).astype(jnp.int32)

    @pl.when(i == 0)
    def _():
        stf_ref[...] = jnp.zeros_like(stf_ref)
        stb_ref[...] = jnp.zeros_like(stb_ref)
        prepare(lff0_ref, lfb0_ref, slot)

    fwd = (qf_ref, vf_ref, kf_ref.at[slot], bf_ref.at[slot], of_ref, stf_ref)
    bwd = (qb_ref, vb_ref, kb_ref.at[slot], bb_ref.at[slot], ob_ref, stb_ref)
    tame = tame_ref[slot] == 1

    @pl.when(tame)
    def _():
        _scan_factorised(*fwd, reverse=False, chunk=chunk, heads=heads)
        _scan_factorised(*bwd, reverse=True, chunk=chunk, heads=heads)
        prepare(lff_ref, lfb_ref, nxt)

    @pl.when(jnp.logical_not(tame))
    def _():
        _scan_direct(*fwd, reverse=False, chunk=chunk, heads=heads)
        _scan_direct(*bwd, reverse=True, chunk=chunk, heads=heads)
        prepare(lff_ref, lfb_ref, nxt)


def _hgrn_scan(q, v, lff, lfb):
    b, s, d = q.shape
    blk = min(SCAN_BLOCK, s)
    chunk = min(SCAN_CHUNK, blk)
    nb = s // blk
    heads = d // HEAD_DIM
    fwd = pl.BlockSpec((None, blk, d), lambda bi, i: (bi, i, 0))
    bwd = pl.BlockSpec((None, blk, d), lambda bi, i: (bi, nb - 1 - i, 0))
    fwd_next = pl.BlockSpec((None, blk, d), lambda bi, i: (bi, jnp.minimum(i + 1, nb - 1), 0))
    bwd_next = pl.BlockSpec((None, blk, d), lambda bi, i: (bi, jnp.maximum(nb - 2 - i, 0), 0))
    fwd_first = pl.BlockSpec((None, blk, d), lambda bi, i: (bi, 0, 0))
    bwd_first = pl.BlockSpec((None, blk, d), lambda bi, i: (bi, nb - 1, 0))
    out = jax.ShapeDtypeStruct((b, s, d), f32)
    state = pltpu.VMEM((heads, HEAD_DIM, HEAD_DIM), f32)
    slots = pltpu.VMEM((2, blk, d), f32)
    return pl.pallas_call(
        functools.partial(_scan_kernel, chunk=chunk, heads=heads),
        out_shape=(out, out),
        grid=(b, nb),
        in_specs=[fwd, fwd, fwd_next, fwd_first, bwd, bwd, bwd_next, bwd_first],
        out_specs=(fwd, bwd),
        scratch_shapes=[state, state, slots, slots, slots, slots, pltpu.SMEM((2,), jnp.int32)],
        compiler_params=_params(2),
        name="hgrn_scan",
    )(q, v, lff, lff, q, v, lfb, lfb)


def kernel(x, norm_g, ffn_w13, ffn_w2, conv_w_pw1, conv_b_pw1, conv_w_dw, conv_b_dw,
           conv_ln_g, conv_ln_b, conv_w_pw2, conv_b_pw2, hgrn_w_in, hgrn_lb,
           hgrn_gn_g, hgrn_w_out, final_g):
    b, s, d = x.shape
    depth = norm_g.shape[0]
    t = b * s
    lb_table = jnp.cumsum(jax.nn.softmax(hgrn_lb.astype(f32), axis=1), axis=1)
    lb_table = lb_table - lb_table[:, :1]
    wb = lambda w: w.astype(bf16)

    x = x.reshape(t, d)
    for layer in range(depth):
        x = _ffn(x, norm_g[layer, 0], ffn_w13[layer, 0], ffn_w2[layer, 0], final_g, final=False)
        j = layer // 2
        hgrn_out = None
        if layer % 2 == 0:
            x = _conv_mixer(x.reshape(b, s, d), norm_g[layer, 1], wb(conv_w_pw1[j]),
                            conv_b_pw1[j], conv_w_dw[j], conv_b_dw[j], conv_ln_g[j],
                            conv_ln_b[j], wb(conv_w_pw2[j]), conv_b_pw2[j]).reshape(t, d)
        else:
            q, v, lff, lfb, sg = _hgrn_proj(
                x, norm_g[layer, 1], hgrn_w_in[j], lb_table[0, layer], lb_table[1, layer])
            r3 = lambda a: a.reshape(b, s, d)
            o_f, o_b = _hgrn_scan(r3(q), r3(v), r3(lff), r3(lfb))
            hgrn_out = (o_f.reshape(t, d), o_b.reshape(t, d), sg, hgrn_gn_g[j], hgrn_w_out[j])
        x = _ffn(x, norm_g[layer, 2], ffn_w13[layer, 1], ffn_w2[layer, 1], final_g,
                 final=(layer == depth - 1), hgrn_out=hgrn_out)
    return x.reshape(b, s, d)
```

```python
import functools

import jax
import jax.numpy as jnp
import numpy as np
from jax import lax
from jax.experimental import pallas as pl
from jax.experimental.pallas import tpu as pltpu

EPS = 1e-6
FFN_RES = 0.5
CONV_WIDTH = 31
CONV_PAD = (CONV_WIDTH - 1) // 2
HEAD_DIM = 128
N_PROJ = 5

SUBLANES = 8
Q_TILE = 2 * SUBLANES
LANES = 128
MXU_COLS = 256
HALO = 2 * SUBLANES
VMEM_LIMIT = 56 * 1024 * 1024

TOKEN_TILE = 512
CAST_STEPS = 16
SCAN_BLOCK = 512
SCAN_CHUNK = 128
SCAN_LANES = 256
CONV_GROUP = 4
CONV_LANES = 512
MAX_CHUNK_DECAY = 80.0

f32 = jnp.float32
bf16 = jnp.bfloat16


def _rms(x, g):
    return x * lax.rsqrt(jnp.mean(x * x, axis=-1, keepdims=True) + EPS) * g


def _resident(shape):
    return pl.BlockSpec(shape, lambda *_: (0,) * len(shape), pipeline_mode=pl.Buffered(1))


def _weight_chunks(stack, which):
    k, n = stack.shape[len(which):]
    rows = k // CAST_STEPS
    assert rows * CAST_STEPS == k and rows % Q_TILE == 0
    return pl.BlockSpec((None,) * len(which) + (rows, n),
                        lambda s: (*which, jnp.minimum(s, CAST_STEPS - 1), 0))


def _cast_chunk(chunk_ref, w_ref, step):
    rows = chunk_ref.shape[0]
    w_ref[pl.ds(pl.multiple_of(step * rows, rows), rows), :] = chunk_ref[...].astype(bf16)


def _params(n_axes):
    return pltpu.CompilerParams(dimension_semantics=("arbitrary",) * n_axes,
                                vmem_limit_bytes=VMEM_LIMIT)


def _ffn_kernel(*refs, d_ff, fc, final, heads):
    if heads:
        (x_ref, of_ref, ob_ref, sg_ref, gn_ref, woc_ref, g_ref, w13c_ref, w2c_ref, fg_ref,
         o_ref, a_ref, w13_ref, w2_ref, y_ref, wo_ref) = refs
    else:
        x_ref, g_ref, w13c_ref, w2c_ref, fg_ref, o_ref, a_ref, w13_ref, w2_ref = refs
    step = pl.program_id(0)

    @pl.when(step < CAST_STEPS)
    def _():
        _cast_chunk(w13c_ref, w13_ref, step)
        _cast_chunk(w2c_ref, w2_ref, step)
        if heads:
            _cast_chunk(woc_ref, wo_ref, step)

    @pl.when(step >= CAST_STEPS)
    def _():
        if heads:
            for h in range(heads):
                hs = slice(h * HEAD_DIM, (h + 1) * HEAD_DIM)
                o = of_ref[:, hs] + ob_ref[:, hs]
                o = o * lax.rsqrt(jnp.mean(o * o, axis=-1, keepdims=True) + EPS)
                y_ref[:, hs] = (o * gn_ref[:, hs] * sg_ref[:, hs]).astype(bf16)
            x = x_ref[...] + jnp.dot(y_ref[...], wo_ref[...], preferred_element_type=f32)
        else:
            x = x_ref[...]
        h = _rms(x, g_ref[...]).astype(bf16)
        for j in range(d_ff // fc):
            gate = jnp.dot(h, w13_ref[:, j * fc:(j + 1) * fc], preferred_element_type=f32)
            up = jnp.dot(h, w13_ref[:, d_ff + j * fc:d_ff + (j + 1) * fc],
                         preferred_element_type=f32)
            a_ref[:, j * fc:(j + 1) * fc] = (jax.nn.silu(gate) * up).astype(bf16)
        y = jnp.dot(a_ref[...], w2_ref[...], preferred_element_type=f32)
        out = x + FFN_RES * y
        if final:
            out = _rms(out, fg_ref[...])
        o_ref[...] = out


def _ffn(x, g, w13_stack, w2_stack, which, final_g, *, final, hgrn_out=None):
    t, d = x.shape
    d_ff = w2_stack.shape[-2]
    tm = min(TOKEN_TILE, t)
    fc = MXU_COLS if d_ff % MXU_COLS == 0 else d_ff
    row = pl.BlockSpec((tm, d), lambda s: (jnp.maximum(s - CAST_STEPS, 0), 0))
    args = [x]
    specs = [row]
    scratch = [pltpu.VMEM((tm, d_ff), bf16), pltpu.VMEM(w13_stack.shape[-2:], bf16),
               pltpu.VMEM(w2_stack.shape[-2:], bf16)]
    if hgrn_out is not None:
        o_f, o_b, sg, gn, wo_stack, wo_which = hgrn_out
        args += [o_f, o_b, sg, gn.reshape(1, d), wo_stack]
        specs += [row, row, row, _resident((1, d)), _weight_chunks(wo_stack, wo_which)]
        scratch += [pltpu.VMEM((tm, d), bf16), pltpu.VMEM(wo_stack.shape[-2:], bf16)]
    args += [g.reshape(1, d), w13_stack, w2_stack, final_g.reshape(1, d)]
    specs += [_resident((1, d)), _weight_chunks(w13_stack, which),
              _weight_chunks(w2_stack, which), _resident((1, d))]
    return pl.pallas_call(
        functools.partial(_ffn_kernel, d_ff=d_ff, fc=fc, final=final,
                          heads=0 if hgrn_out is None else d // HEAD_DIM),
        out_shape=jax.ShapeDtypeStruct((t, d), f32),
        grid=(CAST_STEPS + t // tm,),
        in_specs=specs,
        out_specs=row,
        scratch_shapes=scratch,
        compiler_params=_params(1),
        name="ffn",
    )(*args)


def _interleave_matrices(tm):
    ext = tm + 2 * HALO
    p = ext // SUBLANES
    rho = np.arange(ext)
    to_il = np.zeros((ext, ext), np.float32)
    to_il[rho, (rho % SUBLANES) * p + rho // SUBLANES] = 1.0
    e = np.arange(tm) + HALO
    to_nat = np.zeros((tm, ext), np.float32)
    to_nat[np.arange(tm), (e % p) * SUBLANES + e // p] = 1.0
    return jnp.asarray(to_il, bf16), jnp.asarray(to_nat, bf16)


def _conv_kernel(x_ref, xp_ref, xn_ref, g_ref, pm_ref, qm_ref, w1_ref, b1_ref, wdw_ref, bdw_ref,
                 lng_ref, lnb_ref, w2_ref, b2_ref, o_ref, hp_ref, u_ref, c_ref, *, tm, d):
    i = pl.program_id(1)
    n = pl.num_programs(1)
    ext = tm + 2 * HALO
    p = ext // SUBLANES
    pad = CONV_PAD * SUBLANES
    xe = jnp.concatenate([xp_ref[...], x_ref[...], xn_ref[...]], axis=0)
    h = _rms(xe, g_ref[...]).astype(bf16)
    hp_ref[...] = jnp.dot(pm_ref[...], h, preferred_element_type=f32).astype(bf16)

    rho = lax.broadcasted_iota(jnp.int32, (ext, 1), 0)
    e = (rho % SUBLANES) * p + rho // SUBLANES
    inside = ((e >= HALO) | (i > 0)) & ((e < HALO + tm) | (i < n - 1))

    for cb in range(d // MXU_COLS):
        cs = slice(cb * MXU_COLS, (cb + 1) * MXU_COLS)
        gs = slice(d + cb * MXU_COLS, d + (cb + 1) * MXU_COLS)
        a = jnp.dot(hp_ref[...], w1_ref[:, cs], preferred_element_type=f32) + b1_ref[:, cs]
        gate = jnp.dot(hp_ref[...], w1_ref[:, gs], preferred_element_type=f32) + b1_ref[:, gs]
        u = jnp.where(inside, a * jax.nn.sigmoid(gate), 0.0)
        u_ref[pad:pad + ext, cs] = u
        tail = u[ext - pad:].reshape(CONV_PAD, SUBLANES, MXU_COLS)
        head = u[:pad].reshape(CONV_PAD, SUBLANES, MXU_COLS)
        u_ref[:pad, cs] = pltpu.roll(tail, 1, axis=1).reshape(pad, MXU_COLS)
        u_ref[pad + ext:, cs] = pltpu.roll(head, SUBLANES - 1, axis=1).reshape(pad, MXU_COLS)

    def group(gi, carry):
        base = pl.multiple_of(gi * (CONV_GROUP * SUBLANES), CONV_GROUP * SUBLANES)
        for c in range(d // CONV_LANES):
            cs = slice(c * CONV_LANES, (c + 1) * CONV_LANES)
            taps = {}
            acc = [None] * CONV_GROUP
            for m in range(CONV_GROUP + CONV_WIDTH - 1):
                rows = u_ref[pl.ds(base + m * SUBLANES, SUBLANES), cs]
                for q in range(CONV_GROUP):
                    k = m - q
                    if 0 <= k < CONV_WIDTH:
                        if k not in taps:
                            taps[k] = wdw_ref[k * SUBLANES:(k + 1) * SUBLANES, cs]
                        term = rows * taps[k]
                        acc[q] = term if acc[q] is None else acc[q] + term
            for q in range(CONV_GROUP):
                c_ref[pl.ds(base + q * SUBLANES, SUBLANES), cs] = acc[q] + bdw_ref[:, cs]
        return carry

    lax.fori_loop(0, p // CONV_GROUP, group, 0)

    cv = c_ref[...]
    mu = jnp.mean(cv, axis=-1, keepdims=True)
    cc = cv - mu
    var = jnp.mean(cc * cc, axis=-1, keepdims=True)
    z = cc * lax.rsqrt(var + EPS) * lng_ref[...] + lnb_ref[...]
    z = jax.nn.silu(z).astype(bf16)
    z = jnp.dot(qm_ref[...], z, preferred_element_type=f32).astype(bf16)
    mix = jnp.dot(z, w2_ref[...], preferred_element_type=f32) + b2_ref[...]
    o_ref[...] = x_ref[...] + mix


def _conv_mixer(x, g, w1, b1, wdw, bdw, lng, lnb, w2, b2):
    b, s, d = x.shape
    tm = min(TOKEN_TILE, s)
    nh = tm // HALO
    last = s // HALO - 1
    ext = tm + 2 * HALO
    assert ext % SUBLANES == 0 and (ext // SUBLANES) % CONV_GROUP == 0
    assert ext // SUBLANES >= CONV_PAD and HALO >= CONV_PAD
    pm, qm = _interleave_matrices(tm)
    wdw8 = jnp.repeat(wdw, SUBLANES, axis=0)
    main = pl.BlockSpec((None, tm, d), lambda bi, i: (bi, i, 0))
    prev = pl.BlockSpec((None, HALO, d), lambda bi, i: (bi, jnp.maximum(i * nh - 1, 0), 0))
    nxt = pl.BlockSpec((None, HALO, d), lambda bi, i: (bi, jnp.minimum((i + 1) * nh, last), 0))
    vec = lambda a: a.reshape(1, -1)
    return pl.pallas_call(
        functools.partial(_conv_kernel, tm=tm, d=d),
        out_shape=jax.ShapeDtypeStruct((b, s, d), f32),
        grid=(b, s // tm),
        in_specs=[main, prev, nxt, _resident((1, d)), _resident(pm.shape), _resident(qm.shape),
                  _resident(w1.shape), _resident((1, 2 * d)), _resident(wdw8.shape),
                  _resident((1, d)), _resident((1, d)), _resident((1, d)), _resident(w2.shape),
                  _resident((1, d))],
        out_specs=main,
        scratch_shapes=[pltpu.VMEM((ext, d), bf16),
                        pltpu.VMEM((ext + 2 * CONV_PAD * SUBLANES, d), f32),
                        pltpu.VMEM((ext, d), f32)],
        compiler_params=_params(2),
        name="conv_mixer",
    )(x, x, x, vec(g), pm, qm, w1, vec(b1), wdw8, vec(bdw), vec(lng), vec(lnb), w2, vec(b2))


def _hproj_kernel(x_ref, g_ref, wc_ref, lbf_ref, lbb_ref,
                  q_ref, v_ref, lff_ref, lfb_ref, sg_ref, w_ref, *, d):
    step = pl.program_id(0)

    @pl.when(step < CAST_STEPS)
    def _():
        _cast_chunk(wc_ref, w_ref, step)

    @pl.when(step >= CAST_STEPS)
    def _():
        h = _rms(x_ref[...], g_ref[...]).astype(bf16)

        for c in range(d // MXU_COLS):
            cs = slice(c * MXU_COLS, (c + 1) * MXU_COLS)

            def proj(j):
                return jnp.dot(h, w_ref[:, j * d + c * MXU_COLS:j * d + (c + 1) * MXU_COLS],
                               preferred_element_type=f32)

            q_ref[:, cs] = jax.nn.silu(proj(0)).astype(bf16)
            v_ref[:, cs] = proj(1).astype(bf16)
            for j, lb_ref, lf_ref in ((2, lbf_ref, lff_ref), (3, lbb_ref, lfb_ref)):
                lb = lb_ref[:, cs]
                lf_ref[:, cs] = jnp.log(lb + (1.0 - lb) * jax.nn.sigmoid(proj(j)))
            sg_ref[:, cs] = jax.nn.silu(proj(4))


def _hgrn_proj(x, g, w_stack, which, lb_f, lb_b):
    t, d = x.shape
    tm = min(TOKEN_TILE, t)
    row = pl.BlockSpec((tm, d), lambda s: (jnp.maximum(s - CAST_STEPS, 0), 0))
    sd = lambda dt: jax.ShapeDtypeStruct((t, d), dt)
    return pl.pallas_call(
        functools.partial(_hproj_kernel, d=d),
        out_shape=(sd(bf16), sd(bf16), sd(f32), sd(f32), sd(f32)),
        grid=(CAST_STEPS + t // tm,),
        in_specs=[row, _resident((1, d)), _weight_chunks(w_stack, which), _resident((1, d)),
                  _resident((1, d))],
        out_specs=(row,) * 5,
        scratch_shapes=[pltpu.VMEM(w_stack.shape[-2:], bf16)],
        compiler_params=_params(1),
        name="hgrn_proj",
    )(x, g.reshape(1, d), w_stack, lb_f.reshape(1, d), lb_b.reshape(1, d))


def _chunk_cumsum(lf_ref, b_ref, k_ref, chunk, reverse):
    n, d = lf_ref.shape
    sub = lax.broadcasted_iota(jnp.int32, (SUBLANES, 1), 0)
    peak = jnp.zeros((SUBLANES, SCAN_LANES), f32)
    for c in range(n // chunk):
        for l in range(d // SCAN_LANES):
            ls = slice(l * SCAN_LANES, (l + 1) * SCAN_LANES)
            tiles = range(chunk // SUBLANES)
            carry = None
            for g in (reversed(tiles) if reverse else tiles):
                rs = slice(c * chunk + g * SUBLANES, c * chunk + (g + 1) * SUBLANES)
                t = lf_ref[rs, ls]
                k_ref[rs, ls] = 1.0 - jnp.exp(t)
                shift = 1
                while shift < SUBLANES:
                    if reverse:
                        moved = pltpu.roll(t, SUBLANES - shift, axis=0)
                        keep = sub < SUBLANES - shift
                    else:
                        moved = pltpu.roll(t, shift, axis=0)
                        keep = sub >= shift
                    t = t + jnp.where(keep, moved, 0.0)
                    shift *= 2
                if carry is not None:
                    t = t + carry
                b_ref[rs, ls] = t
                peak = jnp.maximum(peak, jnp.abs(t))
                end = t[:1] if reverse else t[SUBLANES - 1:]
                carry = jnp.broadcast_to(end, (SUBLANES, SCAN_LANES))
    return jnp.max(peak, keepdims=True)


def _scan_factorised(q_ref, v_ref, k_ref, b_ref, o_ref, st_ref, *, reverse, chunk, heads):
    n = q_ref.shape[0]
    ti = lax.broadcasted_iota(jnp.int32, (chunk, chunk), 0)
    si = lax.broadcasted_iota(jnp.int32, (chunk, chunk), 1)
    causal = (si >= ti) if reverse else (si <= ti)
    order = range(n // chunk - 1, -1, -1) if reverse else range(n // chunk)
    for c in order:
        rs = slice(c * chunk, (c + 1) * chunk)
        b = b_ref[rs, :]
        b_end = b[:1] if reverse else b[chunk - 1:]
        mid = 0.5 * b_end
        e_mid = jnp.exp(mid)
        e_end = jnp.exp(b_end)
        qd = q_ref[rs, :] * jnp.exp(b - mid)
        kd = k_ref[rs, :] * jnp.exp(mid - b)
        qs = (qd * e_mid).astype(bf16)
        ks = (kd * e_mid).astype(bf16)
        qd = qd.astype(bf16)
        kd = kd.astype(bf16)
        for h in range(heads):
            hs = slice(h * HEAD_DIM, (h + 1) * HEAD_DIM)
            v = v_ref[rs, hs]
            sc = lax.dot_general(qd[:, hs], kd[:, hs], (((1,), (1,)), ((), ())),
                                 preferred_element_type=f32)
            sc = jnp.where(causal, sc, 0.0).astype(bf16)
            st = st_ref[h]
            o_ref[rs, hs] = (jnp.dot(sc, v, preferred_element_type=f32)
                             + lax.dot_general(qs[:, hs], st.astype(bf16),
                                               (((1,), (1,)), ((), ())),
                                               preferred_element_type=f32))
            st_ref[h] = st * e_end[:, hs] + lax.dot_general(
                v, ks[:, hs], (((0,), (0,)), ((), ())), preferred_element_type=f32)


def _scan_direct(q_ref, v_ref, k_ref, b_ref, o_ref, st_ref, *, reverse, chunk, heads):
    n = q_ref.shape[0]
    nc = n // chunk
    pos = lax.broadcasted_iota(jnp.int32, (chunk, 1), 0)
    tile_pos = lax.broadcasted_iota(jnp.int32, (Q_TILE, 1), 0)

    def chunk_body(ci, carry):
        c = nc - 1 - ci if reverse else ci
        r0 = pl.multiple_of(c * chunk, chunk)
        rs = pl.ds(r0, chunk)

        def row_body(t, carry):
            b = b_ref[rs, :]
            diff = b_ref[pl.ds(r0 + t, 1), :] - b
            seen = (pos >= t) if reverse else (pos <= t)
            t0 = pl.multiple_of((t // Q_TILE) * Q_TILE, Q_TILE)
            q_tile = q_ref[pl.ds(r0 + t0, Q_TILE), :].astype(f32)
            q_row = jnp.sum(jnp.where(tile_pos == t - t0, q_tile, 0.0), axis=0, keepdims=True)
            p = q_row * k_ref[rs, :] * jnp.exp(jnp.where(seen, diff, -jnp.inf))
            outs = []
            for h in range(heads):
                hs = slice(h * HEAD_DIM, (h + 1) * HEAD_DIM)
                sc = jnp.sum(p[:, hs], axis=-1, keepdims=True)
                outs.append(jnp.sum(sc * v_ref[rs, hs].astype(f32), axis=0, keepdims=True))
            o_ref[pl.ds(r0 + t, 1), :] = jnp.concatenate(outs, axis=1)
            return carry

        lax.fori_loop(0, chunk, row_body, 0)
        b = b_ref[rs, :]
        b_end = b[:1] if reverse else b[chunk - 1:]
        e_end = jnp.exp(b_end)
        qs = (q_ref[rs, :] * jnp.exp(b)).astype(bf16)
        ks = (k_ref[rs, :] * jnp.exp(b_end - b)).astype(bf16)
        for h in range(heads):
            hs = slice(h * HEAD_DIM, (h + 1) * HEAD_DIM)
            st = st_ref[h]
            o_ref[rs, hs] += lax.dot_general(qs[:, hs], st.astype(bf16),
                                             (((1,), (1,)), ((), ())),
                                             preferred_element_type=f32)
            st_ref[h] = st * e_end[:, hs] + lax.dot_general(
                v_ref[rs, hs], ks[:, hs], (((0,), (0,)), ((), ())),
                preferred_element_type=f32)
        return carry

    lax.fori_loop(0, nc, chunk_body, 0)


def _scan_kernel(qf_ref, vf_ref, lff_ref, lff0_ref, qb_ref, vb_ref, lfb_ref, lfb0_ref,
                 of_ref, ob_ref, stf_ref, stb_ref, bf_ref, bb_ref, kf_ref, kb_ref, tame_ref,
                 *, chunk, heads):
    i = pl.program_id(1)
    slot = i % 2
    nxt = 1 - slot

    def prepare(lf_f, lf_b, dst):
        decay = jnp.maximum(
            _chunk_cumsum(lf_f, bf_ref.at[dst], kf_ref.at[dst], chunk, reverse=False),
            _chunk_cumsum(lf_b, bb_ref.at[dst], kb_ref.at[dst], chunk, reverse=True))[0, 0]
        tame_ref[dst] = (decay <= MAX_CHUNK_DECAY).astype(jnp.int32)

    @pl.when(i == 0)
    def _():
        stf_ref[...] = jnp.zeros_like(stf_ref)
        stb_ref[...] = jnp.zeros_like(stb_ref)
        prepare(lff0_ref, lfb0_ref, slot)

    fwd = (qf_ref, vf_ref, kf_ref.at[slot], bf_ref.at[slot], of_ref, stf_ref)
    bwd = (qb_ref, vb_ref, kb_ref.at[slot], bb_ref.at[slot], ob_ref, stb_ref)
    tame = tame_ref[slot] == 1

    @pl.when(tame)
    def _():
        _scan_factorised(*fwd, reverse=False, chunk=chunk, heads=heads)
        _scan_factorised(*bwd, reverse=True, chunk=chunk, heads=heads)
        prepare(lff_ref, lfb_ref, nxt)

    @pl.when(jnp.logical_not(tame))
    def _():
        _scan_direct(*fwd, reverse=False, chunk=chunk, heads=heads)
        _scan_direct(*bwd, reverse=True, chunk=chunk, heads=heads)
        prepare(lff_ref, lfb_ref, nxt)


def _hgrn_scan(q, v, lff, lfb):
    b, s, d = q.shape
    blk = min(SCAN_BLOCK, s)
    chunk = min(SCAN_CHUNK, blk)
    nb = s // blk
    heads = d // HEAD_DIM
    fwd = pl.BlockSpec((None, blk, d), lambda bi, i: (bi, i, 0))
    bwd = pl.BlockSpec((None, blk, d), lambda bi, i: (bi, nb - 1 - i, 0))
    fwd_next = pl.BlockSpec((None, blk, d), lambda bi, i: (bi, jnp.minimum(i + 1, nb - 1), 0))
    bwd_next = pl.BlockSpec((None, blk, d), lambda bi, i: (bi, jnp.maximum(nb - 2 - i, 0), 0))
    fwd_first = pl.BlockSpec((None, blk, d), lambda bi, i: (bi, 0, 0))
    bwd_first = pl.BlockSpec((None, blk, d), lambda bi, i: (bi, nb - 1, 0))
    out = jax.ShapeDtypeStruct((b, s, d), f32)
    state = pltpu.VMEM((heads, HEAD_DIM, HEAD_DIM), f32)
    slots = pltpu.VMEM((2, blk, d), f32)
    return pl.pallas_call(
        functools.partial(_scan_kernel, chunk=chunk, heads=heads),
        out_shape=(out, out),
        grid=(b, nb),
        in_specs=[fwd, fwd, fwd_next, fwd_first, bwd, bwd, bwd_next, bwd_first],
        out_specs=(fwd, bwd),
        scratch_shapes=[state, state, slots, slots, slots, slots, pltpu.SMEM((2,), jnp.int32)],
        compiler_params=_params(2),
        name="hgrn_scan",
    )(q, v, lff, lff, q, v, lfb, lfb)


def kernel(x, norm_g, ffn_w13, ffn_w2, conv_w_pw1, conv_b_pw1, conv_w_dw, conv_b_dw,
           conv_ln_g, conv_ln_b, conv_w_pw2, conv_b_pw2, hgrn_w_in, hgrn_lb,
           hgrn_gn_g, hgrn_w_out, final_g):
    b, s, d = x.shape
    depth = norm_g.shape[0]
    t = b * s
    lb_table = jnp.cumsum(jax.nn.softmax(hgrn_lb.astype(f32), axis=1), axis=1)
    lb_table = lb_table - lb_table[:, :1]
    wb = lambda w: w.astype(bf16)

    x = x.reshape(t, d)
    for layer in range(depth):
        x = _ffn(x, norm_g[layer, 0], ffn_w13, ffn_w2, (layer, 0), final_g, final=False)
        j = layer // 2
        hgrn_out = None
        if layer % 2 == 0:
            x = _conv_mixer(x.reshape(b, s, d), norm_g[layer, 1], wb(conv_w_pw1[j]),
                            conv_b_pw1[j], conv_w_dw[j], conv_b_dw[j], conv_ln_g[j],
                            conv_ln_b[j], wb(conv_w_pw2[j]), conv_b_pw2[j]).reshape(t, d)
        else:
            q, v, lff, lfb, sg = _hgrn_proj(
                x, norm_g[layer, 1], hgrn_w_in, (j,), lb_table[0, layer], lb_table[1, layer])
            r3 = lambda a: a.reshape(b, s, d)
            o_f, o_b = _hgrn_scan(r3(q), r3(v), r3(lff), r3(lfb))
            hgrn_out = (o_f.reshape(t, d), o_b.reshape(t, d), sg, hgrn_gn_g[j], hgrn_w_out, (j,))
        x = _ffn(x, norm_g[layer, 2], ffn_w13, ffn_w2, (layer, 1), final_g,
                 final=(layer == depth - 1), hgrn_out=hgrn_out)
    return x.reshape(b, s, d)
```

```python
import functools

import jax
import jax.numpy as jnp
import numpy as np
from jax import lax
from jax.experimental import pallas as pl
from jax.experimental.pallas import tpu as pltpu

EPS = 1e-6
FFN_RES = 0.5
CONV_WIDTH = 31
CONV_PAD = (CONV_WIDTH - 1) // 2
HEAD_DIM = 128
N_PROJ = 5

SUBLANES = 8
Q_TILE = 2 * SUBLANES
LANES = 128
MXU_COLS = 256
HALO = 2 * SUBLANES
VMEM_LIMIT = 56 * 1024 * 1024

TOKEN_TILE = 512
CAST_STEPS = 16
SCAN_BLOCK = 512
SCAN_CHUNK = 128
SCAN_LANES = 256
CONV_GROUP = 4
CONV_LANES = 512
MAX_CHUNK_DECAY = 80.0

f32 = jnp.float32
bf16 = jnp.bfloat16


def _rms(x, g):
    return x * lax.rsqrt(jnp.mean(x * x, axis=-1, keepdims=True) + EPS) * g


def _resident(shape):
    return pl.BlockSpec(shape, lambda *_: (0,) * len(shape), pipeline_mode=pl.Buffered(1))


def _weight_chunks(stack, which):
    k, n = stack.shape[len(which):]
    rows = k // CAST_STEPS
    assert rows * CAST_STEPS == k and rows % Q_TILE == 0
    return pl.BlockSpec((None,) * len(which) + (rows, n),
                        lambda s: (*which, jnp.minimum(s, CAST_STEPS - 1), 0))


def _cast_chunk(chunk_ref, w_ref, step):
    rows = chunk_ref.shape[0]
    w_ref[pl.ds(pl.multiple_of(step * rows, rows), rows), :] = chunk_ref[...].astype(bf16)


def _params(n_axes):
    return pltpu.CompilerParams(dimension_semantics=("arbitrary",) * n_axes,
                                vmem_limit_bytes=VMEM_LIMIT)


def _ffn_kernel(*refs, d_ff, fc, final, heads):
    if heads:
        (x_ref, of_ref, ob_ref, sg_ref, gn_ref, woc_ref, g_ref, w13c_ref, w2c_ref, fg_ref,
         o_ref, a_ref, w13_ref, w2_ref, y_ref, wo_ref) = refs
    else:
        x_ref, g_ref, w13c_ref, w2c_ref, fg_ref, o_ref, a_ref, w13_ref, w2_ref = refs
    step = pl.program_id(0)

    @pl.when(step < CAST_STEPS)
    def _():
        _cast_chunk(w13c_ref, w13_ref, step)
        _cast_chunk(w2c_ref, w2_ref, step)
        if heads:
            _cast_chunk(woc_ref, wo_ref, step)

    @pl.when(step >= CAST_STEPS)
    def _():
        if heads:
            for h in range(heads):
                hs = slice(h * HEAD_DIM, (h + 1) * HEAD_DIM)
                o = of_ref[:, hs] + ob_ref[:, hs]
                o = o * lax.rsqrt(jnp.mean(o * o, axis=-1, keepdims=True) + EPS)
                y_ref[:, hs] = (o * gn_ref[:, hs] * sg_ref[:, hs]).astype(bf16)
            x = x_ref[...] + jnp.dot(y_ref[...], wo_ref[...], preferred_element_type=f32)
        else:
            x = x_ref[...]
        h = _rms(x, g_ref[...]).astype(bf16)
        for j in range(d_ff // fc):
            gate = jnp.dot(h, w13_ref[:, j * fc:(j + 1) * fc], preferred_element_type=f32)
            up = jnp.dot(h, w13_ref[:, d_ff + j * fc:d_ff + (j + 1) * fc],
                         preferred_element_type=f32)
            a_ref[:, j * fc:(j + 1) * fc] = (jax.nn.silu(gate) * up).astype(bf16)
        y = jnp.dot(a_ref[...], w2_ref[...], preferred_element_type=f32)
        out = x + FFN_RES * y
        if final:
            out = _rms(out, fg_ref[...])
        o_ref[...] = out


def _ffn(x, g, w13_stack, w2_stack, which, final_g, *, final, hgrn_out=None):
    t, d = x.shape
    d_ff = w2_stack.shape[-2]
    tm = min(TOKEN_TILE, t)
    fc = MXU_COLS if d_ff % MXU_COLS == 0 else d_ff
    row = pl.BlockSpec((tm, d), lambda s: (jnp.maximum(s - CAST_STEPS, 0), 0))
    args = [x]
    specs = [row]
    scratch = [pltpu.VMEM((tm, d_ff), bf16), pltpu.VMEM(w13_stack.shape[-2:], bf16),
               pltpu.VMEM(w2_stack.shape[-2:], bf16)]
    if hgrn_out is not None:
        o_f, o_b, sg, gn, wo_stack, wo_which = hgrn_out
        args += [o_f, o_b, sg, gn.reshape(1, d), wo_stack]
        specs += [row, row, row, _resident((1, d)), _weight_chunks(wo_stack, wo_which)]
        scratch += [pltpu.VMEM((tm, d), bf16), pltpu.VMEM(wo_stack.shape[-2:], bf16)]
    args += [g.reshape(1, d), w13_stack, w2_stack, final_g.reshape(1, d)]
    specs += [_resident((1, d)), _weight_chunks(w13_stack, which),
              _weight_chunks(w2_stack, which), _resident((1, d))]
    return pl.pallas_call(
        functools.partial(_ffn_kernel, d_ff=d_ff, fc=fc, final=final,
                          heads=0 if hgrn_out is None else d // HEAD_DIM),
        out_shape=jax.ShapeDtypeStruct((t, d), f32),
        grid=(CAST_STEPS + t // tm,),
        in_specs=specs,
        out_specs=row,
        scratch_shapes=scratch,
        compiler_params=_params(1),
        name="ffn",
    )(*args)


def _interleave_matrix(tm):
    p = tm // SUBLANES
    rho = np.arange(tm)
    to_il = np.zeros((tm, tm), np.float32)
    to_il[rho, (rho % SUBLANES) * p + rho // SUBLANES] = 1.0
    return jnp.asarray(to_il, bf16), jnp.asarray(to_il.T, bf16)


def _conv_kernel(x_ref, xp_ref, xn_ref, g_ref, pm_ref, qm_ref, w1_ref, b1_ref, wdw_ref, bdw_ref,
                 lng_ref, lnb_ref, w2_ref, b2_ref, o_ref, hp_ref, u_ref, c_ref, *, tm, d):
    i = pl.program_id(1)
    n = pl.num_programs(1)
    p = tm // SUBLANES
    pad = CONV_PAD * SUBLANES
    h = _rms(x_ref[...], g_ref[...]).astype(bf16)
    hp_ref[:tm, :] = jnp.dot(pm_ref[...], h, preferred_element_type=f32).astype(bf16)
    hp_ref[tm:tm + HALO, :] = _rms(xp_ref[...], g_ref[...]).astype(bf16)
    hp_ref[tm + HALO:, :] = _rms(xn_ref[...], g_ref[...]).astype(bf16)

    row = lax.broadcasted_iota(jnp.int32, (tm + 2 * HALO, 1), 0)
    inside = (row < tm) | ((row < tm + HALO) & (i > 0)) | ((row >= tm + HALO) & (i < n - 1))
    sub = lax.broadcasted_iota(jnp.int32, (1, SUBLANES, 1), 1)

    for cb in range(d // MXU_COLS):
        cs = slice(cb * MXU_COLS, (cb + 1) * MXU_COLS)
        gs = slice(d + cb * MXU_COLS, d + (cb + 1) * MXU_COLS)
        a = jnp.dot(hp_ref[...], w1_ref[:, cs], preferred_element_type=f32) + b1_ref[:, cs]
        gate = jnp.dot(hp_ref[...], w1_ref[:, gs], preferred_element_type=f32) + b1_ref[:, gs]
        u = jnp.where(inside, a * jax.nn.sigmoid(gate), 0.0)
        u_ref[pad:pad + tm, cs] = u[:tm]
        tail = pltpu.roll(u[tm - pad:tm].reshape(CONV_PAD, SUBLANES, MXU_COLS), 1, axis=1)
        head = pltpu.roll(u[:pad].reshape(CONV_PAD, SUBLANES, MXU_COLS), SUBLANES - 1, axis=1)
        before = u[tm + HALO - CONV_PAD:tm + HALO][:, None, :]
        after = u[tm + HALO:tm + HALO + CONV_PAD][:, None, :]
        u_ref[:pad, cs] = jnp.where(sub == 0, before, tail).reshape(pad, MXU_COLS)
        u_ref[pad + tm:, cs] = jnp.where(sub == SUBLANES - 1, after, head).reshape(pad, MXU_COLS)

    def group(gi, carry):
        base = pl.multiple_of(gi * (CONV_GROUP * SUBLANES), CONV_GROUP * SUBLANES)
        for c in range(d // CONV_LANES):
            cs = slice(c * CONV_LANES, (c + 1) * CONV_LANES)
            taps = {}
            acc = [None] * CONV_GROUP
            for m in range(CONV_GROUP + CONV_WIDTH - 1):
                rows = u_ref[pl.ds(base + m * SUBLANES, SUBLANES), cs]
                for q in range(CONV_GROUP):
                    k = m - q
                    if 0 <= k < CONV_WIDTH:
                        if k not in taps:
                            taps[k] = wdw_ref[k * SUBLANES:(k + 1) * SUBLANES, cs]
                        term = rows * taps[k]
                        acc[q] = term if acc[q] is None else acc[q] + term
            for q in range(CONV_GROUP):
                c_ref[pl.ds(base + q * SUBLANES, SUBLANES), cs] = acc[q] + bdw_ref[:, cs]
        return carry

    lax.fori_loop(0, p // CONV_GROUP, group, 0)

    cv = c_ref[...]
    mu = jnp.mean(cv, axis=-1, keepdims=True)
    cc = cv - mu
    var = jnp.mean(cc * cc, axis=-1, keepdims=True)
    z = cc * lax.rsqrt(var + EPS) * lng_ref[...] + lnb_ref[...]
    z = jax.nn.silu(z).astype(bf16)
    z = jnp.dot(qm_ref[...], z, preferred_element_type=f32).astype(bf16)
    mix = jnp.dot(z, w2_ref[...], preferred_element_type=f32) + b2_ref[...]
    o_ref[...] = x_ref[...] + mix


def _conv_mixer(x, g, w1, b1, wdw, bdw, lng, lnb, w2, b2):
    b, s, d = x.shape
    tm = min(TOKEN_TILE, s)
    nh = tm // HALO
    last = s // HALO - 1
    p = tm // SUBLANES
    assert tm % SUBLANES == 0 and p % CONV_GROUP == 0 and p >= CONV_PAD and HALO >= CONV_PAD
    pm, qm = _interleave_matrix(tm)
    wdw8 = jnp.repeat(wdw, SUBLANES, axis=0)
    main = pl.BlockSpec((None, tm, d), lambda bi, i: (bi, i, 0))
    prev = pl.BlockSpec((None, HALO, d), lambda bi, i: (bi, jnp.maximum(i * nh - 1, 0), 0))
    nxt = pl.BlockSpec((None, HALO, d), lambda bi, i: (bi, jnp.minimum((i + 1) * nh, last), 0))
    vec = lambda a: a.reshape(1, -1)
    return pl.pallas_call(
        functools.partial(_conv_kernel, tm=tm, d=d),
        out_shape=jax.ShapeDtypeStruct((b, s, d), f32),
        grid=(b, s // tm),
        in_specs=[main, prev, nxt, _resident((1, d)), _resident(pm.shape), _resident(qm.shape),
                  _resident(w1.shape), _resident((1, 2 * d)), _resident(wdw8.shape),
                  _resident((1, d)), _resident((1, d)), _resident((1, d)), _resident(w2.shape),
                  _resident((1, d))],
        out_specs=main,
        scratch_shapes=[pltpu.VMEM((tm + 2 * HALO, d), bf16),
                        pltpu.VMEM((tm + 2 * CONV_PAD * SUBLANES, d), f32),
                        pltpu.VMEM((tm, d), f32)],
        compiler_params=_params(2),
        name="conv_mixer",
    )(x, x, x, vec(g), pm, qm, w1, vec(b1), wdw8, vec(bdw), vec(lng), vec(lnb), w2, vec(b2))


def _hproj_kernel(x_ref, g_ref, wc_ref, lbf_ref, lbb_ref,
                  q_ref, v_ref, lff_ref, lfb_ref, sg_ref, w_ref, *, d):
    step = pl.program_id(0)

    @pl.when(step < CAST_STEPS)
    def _():
        _cast_chunk(wc_ref, w_ref, step)

    @pl.when(step >= CAST_STEPS)
    def _():
        h = _rms(x_ref[...], g_ref[...]).astype(bf16)

        for c in range(d // MXU_COLS):
            cs = slice(c * MXU_COLS, (c + 1) * MXU_COLS)

            def proj(j):
                return jnp.dot(h, w_ref[:, j * d + c * MXU_COLS:j * d + (c + 1) * MXU_COLS],
                               preferred_element_type=f32)

            q_ref[:, cs] = jax.nn.silu(proj(0)).astype(bf16)
            v_ref[:, cs] = proj(1).astype(bf16)
            for j, lb_ref, lf_ref in ((2, lbf_ref, lff_ref), (3, lbb_ref, lfb_ref)):
                lb = lb_ref[:, cs]
                lf_ref[:, cs] = jnp.log(lb + (1.0 - lb) * jax.nn.sigmoid(proj(j)))
            sg_ref[:, cs] = jax.nn.silu(proj(4))


def _hgrn_proj(x, g, w_stack, which, lb_f, lb_b):
    t, d = x.shape
    tm = min(TOKEN_TILE, t)
    row = pl.BlockSpec((tm, d), lambda s: (jnp.maximum(s - CAST_STEPS, 0), 0))
    sd = lambda dt: jax.ShapeDtypeStruct((t, d), dt)
    return pl.pallas_call(
        functools.partial(_hproj_kernel, d=d),
        out_shape=(sd(bf16), sd(bf16), sd(f32), sd(f32), sd(f32)),
        grid=(CAST_STEPS + t // tm,),
        in_specs=[row, _resident((1, d)), _weight_chunks(w_stack, which), _resident((1, d)),
                  _resident((1, d))],
        out_specs=(row,) * 5,
        scratch_shapes=[pltpu.VMEM(w_stack.shape[-2:], bf16)],
        compiler_params=_params(1),
        name="hgrn_proj",
    )(x, g.reshape(1, d), w_stack, lb_f.reshape(1, d), lb_b.reshape(1, d))


def _chunk_cumsum(lf_ref, b_ref, k_ref, chunk, reverse):
    n, d = lf_ref.shape
    sub = lax.broadcasted_iota(jnp.int32, (SUBLANES, 1), 0)
    peak = jnp.zeros((SUBLANES, SCAN_LANES), f32)
    for c in range(n // chunk):
        for l in range(d // SCAN_LANES):
            ls = slice(l * SCAN_LANES, (l + 1) * SCAN_LANES)
            tiles = range(chunk // SUBLANES)
            carry = None
            for g in (reversed(tiles) if reverse else tiles):
                rs = slice(c * chunk + g * SUBLANES, c * chunk + (g + 1) * SUBLANES)
                t = lf_ref[rs, ls]
                k_ref[rs, ls] = 1.0 - jnp.exp(t)
                shift = 1
                while shift < SUBLANES:
                    if reverse:
                        moved = pltpu.roll(t, SUBLANES - shift, axis=0)
                        keep = sub < SUBLANES - shift
                    else:
                        moved = pltpu.roll(t, shift, axis=0)
                        keep = sub >= shift
                    t = t + jnp.where(keep, moved, 0.0)
                    shift *= 2
                if carry is not None:
                    t = t + carry
                b_ref[rs, ls] = t
                peak = jnp.maximum(peak, jnp.abs(t))
                end = t[:1] if reverse else t[SUBLANES - 1:]
                carry = jnp.broadcast_to(end, (SUBLANES, SCAN_LANES))
    return jnp.max(peak, keepdims=True)


def _scan_factorised(q_ref, v_ref, k_ref, b_ref, o_ref, st_ref, *, reverse, chunk, heads):
    n = q_ref.shape[0]
    ti = lax.broadcasted_iota(jnp.int32, (chunk, chunk), 0)
    si = lax.broadcasted_iota(jnp.int32, (chunk, chunk), 1)
    causal = (si >= ti) if reverse else (si <= ti)
    order = range(n // chunk - 1, -1, -1) if reverse else range(n // chunk)
    for c in order:
        rs = slice(c * chunk, (c + 1) * chunk)
        b = b_ref[rs, :]
        b_end = b[:1] if reverse else b[chunk - 1:]
        mid = 0.5 * b_end
        e_mid = jnp.exp(mid)
        e_end = jnp.exp(b_end)
        e_up = jnp.exp(b - mid)
        qd = q_ref[rs, :] * e_up
        kd = k_ref[rs, :] / e_up
        qs = (qd * e_mid).astype(bf16)
        ks = (kd * e_mid).astype(bf16)
        qd = qd.astype(bf16)
        kd = kd.astype(bf16)
        for h in range(heads):
            hs = slice(h * HEAD_DIM, (h + 1) * HEAD_DIM)
            v = v_ref[rs, hs]
            sc = lax.dot_general(qd[:, hs], kd[:, hs], (((1,), (1,)), ((), ())),
                                 preferred_element_type=f32)
            sc = jnp.where(causal, sc, 0.0).astype(bf16)
            st = st_ref[h]
            o_ref[rs, hs] = (jnp.dot(sc, v, preferred_element_type=f32)
                             + lax.dot_general(qs[:, hs], st.astype(bf16),
                                               (((1,), (1,)), ((), ())),
                                               preferred_element_type=f32))
            st_ref[h] = st * e_end[:, hs] + lax.dot_general(
                v, ks[:, hs], (((0,), (0,)), ((), ())), preferred_element_type=f32)


def _scan_direct(q_ref, v_ref, k_ref, b_ref, o_ref, st_ref, *, reverse, chunk, heads):
    n = q_ref.shape[0]
    nc = n // chunk
    pos = lax.broadcasted_iota(jnp.int32, (chunk, 1), 0)
    tile_pos = lax.broadcasted_iota(jnp.int32, (Q_TILE, 1), 0)

    def chunk_body(ci, carry):
        c = nc - 1 - ci if reverse else ci
        r0 = pl.multiple_of(c * chunk, chunk)
        rs = pl.ds(r0, chunk)

        def row_body(t, carry):
            b = b_ref[rs, :]
            diff = b_ref[pl.ds(r0 + t, 1), :] - b
            seen = (pos >= t) if reverse else (pos <= t)
            t0 = pl.multiple_of((t // Q_TILE) * Q_TILE, Q_TILE)
            q_tile = q_ref[pl.ds(r0 + t0, Q_TILE), :].astype(f32)
            q_row = jnp.sum(jnp.where(tile_pos == t - t0, q_tile, 0.0), axis=0, keepdims=True)
            p = q_row * k_ref[rs, :] * jnp.exp(jnp.where(seen, diff, -jnp.inf))
            outs = []
            for h in range(heads):
                hs = slice(h * HEAD_DIM, (h + 1) * HEAD_DIM)
                sc = jnp.sum(p[:, hs], axis=-1, keepdims=True)
                outs.append(jnp.sum(sc * v_ref[rs, hs].astype(f32), axis=0, keepdims=True))
            o_ref[pl.ds(r0 + t, 1), :] = jnp.concatenate(outs, axis=1)
            return carry

        lax.fori_loop(0, chunk, row_body, 0)
        b = b_ref[rs, :]
        b_end = b[:1] if reverse else b[chunk - 1:]
        e_end = jnp.exp(b_end)
        qs = (q_ref[rs, :] * jnp.exp(b)).astype(bf16)
        ks = (k_ref[rs, :] * jnp.exp(b_end - b)).astype(bf16)
        for h in range(heads):
            hs = slice(h * HEAD_DIM, (h + 1) * HEAD_DIM)
            st = st_ref[h]
            o_ref[rs, hs] += lax.dot_general(qs[:, hs], st.astype(bf16),
                                             (((1,), (1,)), ((), ())),
                                             preferred_element_type=f32)
            st_ref[h] = st * e_end[:, hs] + lax.dot_general(
                v_ref[rs, hs], ks[:, hs], (((0,), (0,)), ((), ())),
                preferred_element_type=f32)
        return carry

    lax.fori_loop(0, nc, chunk_body, 0)


def _scan_kernel(qf_ref, vf_ref, lff_ref, lff0_ref, qb_ref, vb_ref, lfb_ref, lfb0_ref,
                 of_ref, ob_ref, stf_ref, stb_ref, bf_ref, bb_ref, kf_ref, kb_ref, tame_ref,
                 *, chunk, heads):
    i = pl.program_id(1)
    slot = i % 2
    nxt = 1 - slot

    def prepare(lf_f, lf_b, dst):
        decay = jnp.maximum(
            _chunk_cumsum(lf_f, bf_ref.at[dst], kf_ref.at[dst], chunk, reverse=False),
            _chunk_cumsum(lf_b, bb_ref.at[dst], kb_ref.at[dst], chunk, reverse=True))[0, 0]
        tame_ref[dst] = (decay <= MAX_CHUNK_DECAY).astype(jnp.int32)

    @pl.when(i == 0)
    def _():
        stf_ref[...] = jnp.zeros_like(stf_ref)
        stb_ref[...] = jnp.zeros_like(stb_ref)
        prepare(lff0_ref, lfb0_ref, slot)

    fwd = (qf_ref, vf_ref, kf_ref.at[slot], bf_ref.at[slot], of_ref, stf_ref)
    bwd = (qb_ref, vb_ref, kb_ref.at[slot], bb_ref.at[slot], ob_ref, stb_ref)
    tame = tame_ref[slot] == 1

    @pl.when(tame)
    def _():
        _scan_factorised(*fwd, reverse=False, chunk=chunk, heads=heads)
        _scan_factorised(*bwd, reverse=True, chunk=chunk, heads=heads)
        prepare(lff_ref, lfb_ref, nxt)

    @pl.when(jnp.logical_not(tame))
    def _():
        _scan_direct(*fwd, reverse=False, chunk=chunk, heads=heads)
        _scan_direct(*bwd, reverse=True, chunk=chunk, heads=heads)
        prepare(lff_ref, lfb_ref, nxt)


def _hgrn_scan(q, v, lff, lfb):
    b, s, d = q.shape
    blk = min(SCAN_BLOCK, s)
    chunk = min(SCAN_CHUNK, blk)
    nb = s // blk
    heads = d // HEAD_DIM
    fwd = pl.BlockSpec((None, blk, d), lambda bi, i: (bi, i, 0))
    bwd = pl.BlockSpec((None, blk, d), lambda bi, i: (bi, nb - 1 - i, 0))
    fwd_next = pl.BlockSpec((None, blk, d), lambda bi, i: (bi, jnp.minimum(i + 1, nb - 1), 0))
    bwd_next = pl.BlockSpec((None, blk, d), lambda bi, i: (bi, jnp.maximum(nb - 2 - i, 0), 0))
    fwd_first = pl.BlockSpec((None, blk, d), lambda bi, i: (bi, 0, 0))
    bwd_first = pl.BlockSpec((None, blk, d), lambda bi, i: (bi, nb - 1, 0))
    out = jax.ShapeDtypeStruct((b, s, d), f32)
    state = pltpu.VMEM((heads, HEAD_DIM, HEAD_DIM), f32)
    slots = pltpu.VMEM((2, blk, d), f32)
    return pl.pallas_call(
        functools.partial(_scan_kernel, chunk=chunk, heads=heads),
        out_shape=(out, out),
        grid=(b, nb),
        in_specs=[fwd, fwd, fwd_next, fwd_first, bwd, bwd, bwd_next, bwd_first],
        out_specs=(fwd, bwd),
        scratch_shapes=[state, state, slots, slots, slots, slots, pltpu.SMEM((2,), jnp.int32)],
        compiler_params=_params(2),
        name="hgrn_scan",
    )(q, v, lff, lff, q, v, lfb, lfb)


def kernel(x, norm_g, ffn_w13, ffn_w2, conv_w_pw1, conv_b_pw1, conv_w_dw, conv_b_dw,
           conv_ln_g, conv_ln_b, conv_w_pw2, conv_b_pw2, hgrn_w_in, hgrn_lb,
           hgrn_gn_g, hgrn_w_out, final_g):
    b, s, d = x.shape
    depth = norm_g.shape[0]
    t = b * s
    lb_table = jnp.cumsum(jax.nn.softmax(hgrn_lb.astype(f32), axis=1), axis=1)
    lb_table = lb_table - lb_table[:, :1]
    wb = lambda w: w.astype(bf16)

    x = x.reshape(t, d)
    for layer in range(depth):
        x = _ffn(x, norm_g[layer, 0], ffn_w13, ffn_w2, (layer, 0), final_g, final=False)
        j = layer // 2
        hgrn_out = None
        if layer % 2 == 0:
            x = _conv_mixer(x.reshape(b, s, d), norm_g[layer, 1], wb(conv_w_pw1[j]),
                            conv_b_pw1[j], conv_w_dw[j], conv_b_dw[j], conv_ln_g[j],
                            conv_ln_b[j], wb(conv_w_pw2[j]), conv_b_pw2[j]).reshape(t, d)
        else:
            q, v, lff, lfb, sg = _hgrn_proj(
                x, norm_g[layer, 1], hgrn_w_in, (j,), lb_table[0, layer], lb_table[1, layer])
            r3 = lambda a: a.reshape(b, s, d)
            o_f, o_b = _hgrn_scan(r3(q), r3(v), r3(lff), r3(lfb))
            hgrn_out = (o_f.reshape(t, d), o_b.reshape(t, d), sg, hgrn_gn_g[j], hgrn_w_out, (j,))
        x = _ffn(x, norm_g[layer, 2], ffn_w13, ffn_w2, (layer, 1), final_g,
                 final=(layer == depth - 1), hgrn_out=hgrn_out)
    return x.reshape(b, s, d)
```

```python
import functools

import jax
import jax.numpy as jnp
import numpy as np
from jax import lax
from jax.experimental import pallas as pl
from jax.experimental.pallas import tpu as pltpu

EPS = 1e-6
FFN_RES = 0.5
CONV_WIDTH = 31
CONV_PAD = (CONV_WIDTH - 1) // 2
HEAD_DIM = 128
N_PROJ = 5

SUBLANES = 8
Q_TILE = 2 * SUBLANES
LANES = 128
MXU_COLS = 256
HALO = 2 * SUBLANES
VMEM_LIMIT = 56 * 1024 * 1024

TOKEN_TILE = 512
CAST_STEPS = 16
SCAN_BLOCK = 512
SCAN_CHUNK = 128
SCAN_LANES = 256
CONV_GROUP = 4
CONV_LANES = 512
MAX_CHUNK_DECAY = 80.0

f32 = jnp.float32
bf16 = jnp.bfloat16


def _rms(x, g):
    return x * lax.rsqrt(jnp.mean(x * x, axis=-1, keepdims=True) + EPS) * g


def _resident(shape):
    return pl.BlockSpec(shape, lambda *_: (0,) * len(shape), pipeline_mode=pl.Buffered(1))


def _weight_chunks(stack, which):
    k, n = stack.shape[len(which):]
    rows = k // CAST_STEPS
    assert rows * CAST_STEPS == k and rows % Q_TILE == 0
    return pl.BlockSpec((None,) * len(which) + (rows, n),
                        lambda s: (*which, jnp.minimum(s, CAST_STEPS - 1), 0))


def _cast_chunk(chunk_ref, w_ref, step):
    rows = chunk_ref.shape[0]
    w_ref[pl.ds(pl.multiple_of(step * rows, rows), rows), :] = chunk_ref[...].astype(bf16)


def _params(n_axes):
    return pltpu.CompilerParams(dimension_semantics=("arbitrary",) * n_axes,
                                vmem_limit_bytes=VMEM_LIMIT)


def _ffn_kernel(*refs, d_ff, fc, final, heads):
    if heads:
        (x_ref, of_ref, ob_ref, sg_ref, gn_ref, woc_ref, g_ref, w13c_ref, w2c_ref, fg_ref,
         o_ref, a_ref, w13_ref, w2_ref, y_ref, wo_ref) = refs
    else:
        x_ref, g_ref, w13c_ref, w2c_ref, fg_ref, o_ref, a_ref, w13_ref, w2_ref = refs
    step = pl.program_id(0)

    @pl.when(step < CAST_STEPS)
    def _():
        _cast_chunk(w13c_ref, w13_ref, step)
        _cast_chunk(w2c_ref, w2_ref, step)
        if heads:
            _cast_chunk(woc_ref, wo_ref, step)

    @pl.when(step >= CAST_STEPS)
    def _():
        if heads:
            for h in range(heads):
                hs = slice(h * HEAD_DIM, (h + 1) * HEAD_DIM)
                o = of_ref[:, hs].astype(f32) + ob_ref[:, hs].astype(f32)
                o = o * lax.rsqrt(jnp.mean(o * o, axis=-1, keepdims=True) + EPS)
                y_ref[:, hs] = (o * gn_ref[:, hs] * sg_ref[:, hs]).astype(bf16)
            x = x_ref[...] + jnp.dot(y_ref[...], wo_ref[...], preferred_element_type=f32)
        else:
            x = x_ref[...]
        h = _rms(x, g_ref[...]).astype(bf16)
        for j in range(d_ff // fc):
            gate = jnp.dot(h, w13_ref[:, j * fc:(j + 1) * fc], preferred_element_type=f32)
            up = jnp.dot(h, w13_ref[:, d_ff + j * fc:d_ff + (j + 1) * fc],
                         preferred_element_type=f32)
            a_ref[:, j * fc:(j + 1) * fc] = (jax.nn.silu(gate) * up).astype(bf16)
        y = jnp.dot(a_ref[...], w2_ref[...], preferred_element_type=f32)
        out = x + FFN_RES * y
        if final:
            out = _rms(out, fg_ref[...])
        o_ref[...] = out


def _ffn(x, g, w13_stack, w2_stack, which, final_g, *, final, hgrn_out=None):
    t, d = x.shape
    d_ff = w2_stack.shape[-2]
    tm = min(TOKEN_TILE, t)
    fc = MXU_COLS if d_ff % MXU_COLS == 0 else d_ff
    row = pl.BlockSpec((tm, d), lambda s: (jnp.maximum(s - CAST_STEPS, 0), 0))
    args = [x]
    specs = [row]
    scratch = [pltpu.VMEM((tm, d_ff), bf16), pltpu.VMEM(w13_stack.shape[-2:], bf16),
               pltpu.VMEM(w2_stack.shape[-2:], bf16)]
    if hgrn_out is not None:
        o_f, o_b, sg, gn, wo_stack, wo_which = hgrn_out
        args += [o_f, o_b, sg, gn.reshape(1, d), wo_stack]
        specs += [row, row, row, _resident((1, d)), _weight_chunks(wo_stack, wo_which)]
        scratch += [pltpu.VMEM((tm, d), bf16), pltpu.VMEM(wo_stack.shape[-2:], bf16)]
    args += [g.reshape(1, d), w13_stack, w2_stack, final_g.reshape(1, d)]
    specs += [_resident((1, d)), _weight_chunks(w13_stack, which),
              _weight_chunks(w2_stack, which), _resident((1, d))]
    return pl.pallas_call(
        functools.partial(_ffn_kernel, d_ff=d_ff, fc=fc, final=final,
                          heads=0 if hgrn_out is None else d // HEAD_DIM),
        out_shape=jax.ShapeDtypeStruct((t, d), f32),
        grid=(CAST_STEPS + t // tm,),
        in_specs=specs,
        out_specs=row,
        scratch_shapes=scratch,
        compiler_params=_params(1),
        name="ffn",
    )(*args)


def _interleave_matrix(tm):
    p = tm // SUBLANES
    rho = np.arange(tm)
    to_il = np.zeros((tm, tm), np.float32)
    to_il[rho, (rho % SUBLANES) * p + rho // SUBLANES] = 1.0
    return jnp.asarray(to_il, bf16), jnp.asarray(to_il.T, bf16)


def _conv_kernel(x_ref, xp_ref, xn_ref, g_ref, pm_ref, qm_ref, w1_ref, b1_ref, wdw_ref, bdw_ref,
                 lng_ref, lnb_ref, w2_ref, b2_ref, o_ref, hp_ref, u_ref, c_ref, *, tm, d):
    i = pl.program_id(1)
    n = pl.num_programs(1)
    p = tm // SUBLANES
    pad = CONV_PAD * SUBLANES
    h = _rms(x_ref[...], g_ref[...]).astype(bf16)
    hp_ref[:tm, :] = jnp.dot(pm_ref[...], h, preferred_element_type=f32).astype(bf16)
    hp_ref[tm:tm + HALO, :] = _rms(xp_ref[...], g_ref[...]).astype(bf16)
    hp_ref[tm + HALO:, :] = _rms(xn_ref[...], g_ref[...]).astype(bf16)

    row = lax.broadcasted_iota(jnp.int32, (tm + 2 * HALO, 1), 0)
    inside = (row < tm) | ((row < tm + HALO) & (i > 0)) | ((row >= tm + HALO) & (i < n - 1))
    sub = lax.broadcasted_iota(jnp.int32, (1, SUBLANES, 1), 1)

    for cb in range(d // MXU_COLS):
        cs = slice(cb * MXU_COLS, (cb + 1) * MXU_COLS)
        gs = slice(d + cb * MXU_COLS, d + (cb + 1) * MXU_COLS)
        a = jnp.dot(hp_ref[...], w1_ref[:, cs], preferred_element_type=f32) + b1_ref[:, cs]
        gate = jnp.dot(hp_ref[...], w1_ref[:, gs], preferred_element_type=f32) + b1_ref[:, gs]
        u = jnp.where(inside, a * jax.nn.sigmoid(gate), 0.0)
        u_ref[pad:pad + tm, cs] = u[:tm]
        tail = pltpu.roll(u[tm - pad:tm].reshape(CONV_PAD, SUBLANES, MXU_COLS), 1, axis=1)
        head = pltpu.roll(u[:pad].reshape(CONV_PAD, SUBLANES, MXU_COLS), SUBLANES - 1, axis=1)
        before = u[tm + HALO - CONV_PAD:tm + HALO][:, None, :]
        after = u[tm + HALO:tm + HALO + CONV_PAD][:, None, :]
        u_ref[:pad, cs] = jnp.where(sub == 0, before, tail).reshape(pad, MXU_COLS)
        u_ref[pad + tm:, cs] = jnp.where(sub == SUBLANES - 1, after, head).reshape(pad, MXU_COLS)

    def group(gi, carry):
        base = pl.multiple_of(gi * (CONV_GROUP * SUBLANES), CONV_GROUP * SUBLANES)
        for c in range(d // CONV_LANES):
            cs = slice(c * CONV_LANES, (c + 1) * CONV_LANES)
            taps = {}
            acc = [None] * CONV_GROUP
            for m in range(CONV_GROUP + CONV_WIDTH - 1):
                rows = u_ref[pl.ds(base + m * SUBLANES, SUBLANES), cs]
                for q in range(CONV_GROUP):
                    k = m - q
                    if 0 <= k < CONV_WIDTH:
                        if k not in taps:
                            taps[k] = wdw_ref[k * SUBLANES:(k + 1) * SUBLANES, cs]
                        term = rows * taps[k]
                        acc[q] = term if acc[q] is None else acc[q] + term
            for q in range(CONV_GROUP):
                c_ref[pl.ds(base + q * SUBLANES, SUBLANES), cs] = acc[q] + bdw_ref[:, cs]
        return carry

    lax.fori_loop(0, p // CONV_GROUP, group, 0)

    cv = c_ref[...]
    mu = jnp.mean(cv, axis=-1, keepdims=True)
    cc = cv - mu
    var = jnp.mean(cc * cc, axis=-1, keepdims=True)
    z = cc * lax.rsqrt(var + EPS) * lng_ref[...] + lnb_ref[...]
    z = jax.nn.silu(z).astype(bf16)
    z = jnp.dot(qm_ref[...], z, preferred_element_type=f32).astype(bf16)
    mix = jnp.dot(z, w2_ref[...], preferred_element_type=f32) + b2_ref[...]
    o_ref[...] = x_ref[...] + mix


def _conv_mixer(x, g, w1, b1, wdw, bdw, lng, lnb, w2, b2):
    b, s, d = x.shape
    tm = min(TOKEN_TILE, s)
    nh = tm // HALO
    last = s // HALO - 1
    p = tm // SUBLANES
    assert tm % SUBLANES == 0 and p % CONV_GROUP == 0 and p >= CONV_PAD and HALO >= CONV_PAD
    pm, qm = _interleave_matrix(tm)
    wdw8 = jnp.repeat(wdw, SUBLANES, axis=0)
    main = pl.BlockSpec((None, tm, d), lambda bi, i: (bi, i, 0))
    prev = pl.BlockSpec((None, HALO, d), lambda bi, i: (bi, jnp.maximum(i * nh - 1, 0), 0))
    nxt = pl.BlockSpec((None, HALO, d), lambda bi, i: (bi, jnp.minimum((i + 1) * nh, last), 0))
    vec = lambda a: a.reshape(1, -1)
    return pl.pallas_call(
        functools.partial(_conv_kernel, tm=tm, d=d),
        out_shape=jax.ShapeDtypeStruct((b, s, d), f32),
        grid=(b, s // tm),
        in_specs=[main, prev, nxt, _resident((1, d)), _resident(pm.shape), _resident(qm.shape),
                  _resident(w1.shape), _resident((1, 2 * d)), _resident(wdw8.shape),
                  _resident((1, d)), _resident((1, d)), _resident((1, d)), _resident(w2.shape),
                  _resident((1, d))],
        out_specs=main,
        scratch_shapes=[pltpu.VMEM((tm + 2 * HALO, d), bf16),
                        pltpu.VMEM((tm + 2 * CONV_PAD * SUBLANES, d), f32),
                        pltpu.VMEM((tm, d), f32)],
        compiler_params=_params(2),
        name="conv_mixer",
    )(x, x, x, vec(g), pm, qm, w1, vec(b1), wdw8, vec(bdw), vec(lng), vec(lnb), w2, vec(b2))


def _hproj_kernel(x_ref, g_ref, wc_ref, lbf_ref, lbb_ref,
                  q_ref, v_ref, lff_ref, lfb_ref, sg_ref, w_ref, *, d):
    step = pl.program_id(0)

    @pl.when(step < CAST_STEPS)
    def _():
        _cast_chunk(wc_ref, w_ref, step)

    @pl.when(step >= CAST_STEPS)
    def _():
        h = _rms(x_ref[...], g_ref[...]).astype(bf16)

        for c in range(d // MXU_COLS):
            cs = slice(c * MXU_COLS, (c + 1) * MXU_COLS)

            def proj(j):
                return jnp.dot(h, w_ref[:, j * d + c * MXU_COLS:j * d + (c + 1) * MXU_COLS],
                               preferred_element_type=f32)

            q_ref[:, cs] = jax.nn.silu(proj(0)).astype(bf16)
            v_ref[:, cs] = proj(1).astype(bf16)
            for j, lb_ref, lf_ref in ((2, lbf_ref, lff_ref), (3, lbb_ref, lfb_ref)):
                lb = lb_ref[:, cs]
                lf_ref[:, cs] = jnp.log(lb + (1.0 - lb) * jax.nn.sigmoid(proj(j)))
            sg_ref[:, cs] = jax.nn.silu(proj(4)).astype(bf16)


def _hgrn_proj(x, g, w_stack, which, lb_f, lb_b):
    t, d = x.shape
    tm = min(2 * TOKEN_TILE, t)
    row = pl.BlockSpec((tm, d), lambda s: (jnp.maximum(s - CAST_STEPS, 0), 0))
    sd = lambda dt: jax.ShapeDtypeStruct((t, d), dt)
    return pl.pallas_call(
        functools.partial(_hproj_kernel, d=d),
        out_shape=(sd(bf16), sd(bf16), sd(f32), sd(f32), sd(bf16)),
        grid=(CAST_STEPS + t // tm,),
        in_specs=[row, _resident((1, d)), _weight_chunks(w_stack, which), _resident((1, d)),
                  _resident((1, d))],
        out_specs=(row,) * 5,
        scratch_shapes=[pltpu.VMEM(w_stack.shape[-2:], bf16)],
        compiler_params=_params(1),
        name="hgrn_proj",
    )(x, g.reshape(1, d), w_stack, lb_f.reshape(1, d), lb_b.reshape(1, d))


def _chunk_cumsum(lf_ref, b_ref, k_ref, chunk, reverse):
    n, d = lf_ref.shape
    sub = lax.broadcasted_iota(jnp.int32, (SUBLANES, 1), 0)
    peak = jnp.zeros((SUBLANES, SCAN_LANES), f32)
    for c in range(n // chunk):
        for l in range(d // SCAN_LANES):
            ls = slice(l * SCAN_LANES, (l + 1) * SCAN_LANES)
            tiles = range(chunk // SUBLANES)
            carry = None
            for g in (reversed(tiles) if reverse else tiles):
                rs = slice(c * chunk + g * SUBLANES, c * chunk + (g + 1) * SUBLANES)
                t = lf_ref[rs, ls]
                k_ref[rs, ls] = 1.0 - jnp.exp(t)
                shift = 1
                while shift < SUBLANES:
                    if reverse:
                        moved = pltpu.roll(t, SUBLANES - shift, axis=0)
                        keep = sub < SUBLANES - shift
                    else:
                        moved = pltpu.roll(t, shift, axis=0)
                        keep = sub >= shift
                    t = t + jnp.where(keep, moved, 0.0)
                    shift *= 2
                if carry is not None:
                    t = t + carry
                b_ref[rs, ls] = t
                peak = jnp.maximum(peak, jnp.abs(t))
                end = t[:1] if reverse else t[SUBLANES - 1:]
                carry = jnp.broadcast_to(end, (SUBLANES, SCAN_LANES))
    return jnp.max(peak, keepdims=True)


def _scan_factorised(q_ref, v_ref, k_ref, b_ref, o_ref, st_ref, *, reverse, chunk, heads):
    n = q_ref.shape[0]
    ti = lax.broadcasted_iota(jnp.int32, (chunk, chunk), 0)
    si = lax.broadcasted_iota(jnp.int32, (chunk, chunk), 1)
    causal = (si >= ti) if reverse else (si <= ti)
    order = range(n // chunk - 1, -1, -1) if reverse else range(n // chunk)
    for c in order:
        rs = slice(c * chunk, (c + 1) * chunk)
        b = b_ref[rs, :]
        b_end = b[:1] if reverse else b[chunk - 1:]
        mid = 0.5 * b_end
        e_mid = jnp.exp(mid)
        e_end = jnp.exp(b_end)
        e_up = jnp.exp(b - mid)
        qd = q_ref[rs, :] * e_up
        kd = k_ref[rs, :] / e_up
        qs = (qd * e_mid).astype(bf16)
        ks = (kd * e_mid).astype(bf16)
        qd = qd.astype(bf16)
        kd = kd.astype(bf16)
        for h in range(heads):
            hs = slice(h * HEAD_DIM, (h + 1) * HEAD_DIM)
            v = v_ref[rs, hs]
            sc = lax.dot_general(qd[:, hs], kd[:, hs], (((1,), (1,)), ((), ())),
                                 preferred_element_type=f32)
            sc = jnp.where(causal, sc, 0.0).astype(bf16)
            st = st_ref[h]
            o_ref[rs, hs] = (jnp.dot(sc, v, preferred_element_type=f32)
                             + lax.dot_general(qs[:, hs], st.astype(bf16),
                                               (((1,), (1,)), ((), ())),
                                               preferred_element_type=f32)).astype(o_ref.dtype)
            st_ref[h] = st * e_end[:, hs] + lax.dot_general(
                v, ks[:, hs], (((0,), (0,)), ((), ())), preferred_element_type=f32)


def _scan_direct(q_ref, v_ref, k_ref, b_ref, o_ref, st_ref, tmp_ref, *, reverse, chunk, heads):
    n = q_ref.shape[0]
    nc = n // chunk
    pos = lax.broadcasted_iota(jnp.int32, (chunk, 1), 0)
    tile_pos = lax.broadcasted_iota(jnp.int32, (Q_TILE, 1), 0)

    def chunk_body(ci, carry):
        c = nc - 1 - ci if reverse else ci
        r0 = pl.multiple_of(c * chunk, chunk)
        rs = pl.ds(r0, chunk)

        def row_body(t, carry):
            b = b_ref[rs, :]
            diff = b_ref[pl.ds(r0 + t, 1), :] - b
            seen = (pos >= t) if reverse else (pos <= t)
            t0 = pl.multiple_of((t // Q_TILE) * Q_TILE, Q_TILE)
            q_tile = q_ref[pl.ds(r0 + t0, Q_TILE), :].astype(f32)
            q_row = jnp.sum(jnp.where(tile_pos == t - t0, q_tile, 0.0), axis=0, keepdims=True)
            p = q_row * k_ref[rs, :] * jnp.exp(jnp.where(seen, diff, -jnp.inf))
            outs = []
            for h in range(heads):
                hs = slice(h * HEAD_DIM, (h + 1) * HEAD_DIM)
                sc = jnp.sum(p[:, hs], axis=-1, keepdims=True)
                outs.append(jnp.sum(sc * v_ref[rs, hs].astype(f32), axis=0, keepdims=True))
            tmp_ref[pl.ds(r0 + t, 1), :] = jnp.concatenate(outs, axis=1)
            return carry

        lax.fori_loop(0, chunk, row_body, 0)
        b = b_ref[rs, :]
        b_end = b[:1] if reverse else b[chunk - 1:]
        e_end = jnp.exp(b_end)
        qs = (q_ref[rs, :] * jnp.exp(b)).astype(bf16)
        ks = (k_ref[rs, :] * jnp.exp(b_end - b)).astype(bf16)
        for h in range(heads):
            hs = slice(h * HEAD_DIM, (h + 1) * HEAD_DIM)
            st = st_ref[h]
            o_ref[rs, hs] = (tmp_ref[rs, hs]
                             + lax.dot_general(qs[:, hs], st.astype(bf16),
                                               (((1,), (1,)), ((), ())),
                                               preferred_element_type=f32)).astype(o_ref.dtype)
            st_ref[h] = st * e_end[:, hs] + lax.dot_general(
                v_ref[rs, hs], ks[:, hs], (((0,), (0,)), ((), ())),
                preferred_element_type=f32)
        return carry

    lax.fori_loop(0, nc, chunk_body, 0)


def _scan_kernel(qf_ref, vf_ref, lff_ref, lff0_ref, qb_ref, vb_ref, lfb_ref, lfb0_ref,
                 of_ref, ob_ref, stf_ref, stb_ref, bf_ref, bb_ref, kf_ref, kb_ref, tmp_ref,
                 tame_ref, *, chunk, heads):
    i = pl.program_id(1)
    slot = i % 2
    nxt = 1 - slot

    def prepare(lf_f, lf_b, dst):
        decay = jnp.maximum(
            _chunk_cumsum(lf_f, bf_ref.at[dst], kf_ref.at[dst], chunk, reverse=False),
            _chunk_cumsum(lf_b, bb_ref.at[dst], kb_ref.at[dst], chunk, reverse=True))[0, 0]
        tame_ref[dst] = (decay <= MAX_CHUNK_DECAY).astype(jnp.int32)

    @pl.when(i == 0)
    def _():
        stf_ref[...] = jnp.zeros_like(stf_ref)
        stb_ref[...] = jnp.zeros_like(stb_ref)
        prepare(lff0_ref, lfb0_ref, slot)

    fwd = (qf_ref, vf_ref, kf_ref.at[slot], bf_ref.at[slot], of_ref, stf_ref)
    bwd = (qb_ref, vb_ref, kb_ref.at[slot], bb_ref.at[slot], ob_ref, stb_ref)
    tame = tame_ref[slot] == 1

    @pl.when(tame)
    def _():
        _scan_factorised(*fwd, reverse=False, chunk=chunk, heads=heads)
        _scan_factorised(*bwd, reverse=True, chunk=chunk, heads=heads)
        prepare(lff_ref, lfb_ref, nxt)

    @pl.when(jnp.logical_not(tame))
    def _():
        _scan_direct(*fwd, tmp_ref, reverse=False, chunk=chunk, heads=heads)
        _scan_direct(*bwd, tmp_ref, reverse=True, chunk=chunk, heads=heads)
        prepare(lff_ref, lfb_ref, nxt)


def _hgrn_scan(q, v, lff, lfb):
    b, s, d = q.shape
    blk = min(SCAN_BLOCK, s)
    chunk = min(SCAN_CHUNK, blk)
    nb = s // blk
    heads = d // HEAD_DIM
    fwd = pl.BlockSpec((None, blk, d), lambda bi, i: (bi, i, 0))
    bwd = pl.BlockSpec((None, blk, d), lambda bi, i: (bi, nb - 1 - i, 0))
    fwd_next = pl.BlockSpec((None, blk, d), lambda bi, i: (bi, jnp.minimum(i + 1, nb - 1), 0))
    bwd_next = pl.BlockSpec((None, blk, d), lambda bi, i: (bi, jnp.maximum(nb - 2 - i, 0), 0))
    fwd_first = pl.BlockSpec((None, blk, d), lambda bi, i: (bi, 0, 0))
    bwd_first = pl.BlockSpec((None, blk, d), lambda bi, i: (bi, nb - 1, 0))
    out = jax.ShapeDtypeStruct((b, s, d), bf16)
    state = pltpu.VMEM((heads, HEAD_DIM, HEAD_DIM), f32)
    slots = pltpu.VMEM((2, blk, d), f32)
    return pl.pallas_call(
        functools.partial(_scan_kernel, chunk=chunk, heads=heads),
        out_shape=(out, out),
        grid=(b, nb),
        in_specs=[fwd, fwd, fwd_next, fwd_first, bwd, bwd, bwd_next, bwd_first],
        out_specs=(fwd, bwd),
        scratch_shapes=[state, state, slots, slots, slots, slots, pltpu.VMEM((blk, d), f32),
                        pltpu.SMEM((2,), jnp.int32)],
        compiler_params=_params(2),
        name="hgrn_scan",
    )(q, v, lff, lff, q, v, lfb, lfb)


def kernel(x, norm_g, ffn_w13, ffn_w2, conv_w_pw1, conv_b_pw1, conv_w_dw, conv_b_dw,
           conv_ln_g, conv_ln_b, conv_w_pw2, conv_b_pw2, hgrn_w_in, hgrn_lb,
           hgrn_gn_g, hgrn_w_out, final_g):
    b, s, d = x.shape
    depth = norm_g.shape[0]
    t = b * s
    lb_table = jnp.cumsum(jax.nn.softmax(hgrn_lb.astype(f32), axis=1), axis=1)
    lb_table = lb_table - lb_table[:, :1]
    wb = lambda w: w.astype(bf16)

    x = x.reshape(t, d)
    for layer in range(depth):
        x = _ffn(x, norm_g[layer, 0], ffn_w13, ffn_w2, (layer, 0), final_g, final=False)
        j = layer // 2
        hgrn_out = None
        if layer % 2 == 0:
            x = _conv_mixer(x.reshape(b, s, d), norm_g[layer, 1], wb(conv_w_pw1[j]),
                            conv_b_pw1[j], conv_w_dw[j], conv_b_dw[j], conv_ln_g[j],
                            conv_ln_b[j], wb(conv_w_pw2[j]), conv_b_pw2[j]).reshape(t, d)
        else:
            q, v, lff, lfb, sg = _hgrn_proj(
                x, norm_g[layer, 1], hgrn_w_in, (j,), lb_table[0, layer], lb_table[1, layer])
            r3 = lambda a: a.reshape(b, s, d)
            o_f, o_b = _hgrn_scan(r3(q), r3(v), r3(lff), r3(lfb))
            hgrn_out = (o_f.reshape(t, d), o_b.reshape(t, d), sg, hgrn_gn_g[j], hgrn_w_out, (j,))
        x = _ffn(x, norm_g[layer, 2], ffn_w13, ffn_w2, (layer, 1), final_g,
                 final=(layer == depth - 1), hgrn_out=hgrn_out)
    return x.reshape(b, s, d)
```

```python
import functools

import jax
import jax.numpy as jnp
import numpy as np
from jax import lax
from jax.experimental import pallas as pl
from jax.experimental.pallas import tpu as pltpu

EPS = 1e-6
FFN_RES = 0.5
CONV_WIDTH = 31
CONV_PAD = (CONV_WIDTH - 1) // 2
HEAD_DIM = 128
N_PROJ = 5

SUBLANES = 8
Q_TILE = 2 * SUBLANES
LANES = 128
MXU_COLS = 256
HALO = 2 * SUBLANES
VMEM_LIMIT = 56 * 1024 * 1024

TOKEN_TILE = 512
PROJ_TILE = 1024
CAST_STEPS = 16
SCAN_BLOCK = 512
SCAN_CHUNK = 128
SCAN_LANES = 256
CONV_GROUP = 4
CONV_LANES = 512
MAX_CHUNK_DECAY = 80.0

f32 = jnp.float32
bf16 = jnp.bfloat16


def _rms(x, g):
    return x * lax.rsqrt(jnp.mean(x * x, axis=-1, keepdims=True) + EPS) * g


def _resident(shape):
    return pl.BlockSpec(shape, lambda *_: (0,) * len(shape), pipeline_mode=pl.Buffered(1))


def _weight_chunks(stack, which):
    k, n = stack.shape[len(which):]
    rows = k // CAST_STEPS
    assert rows * CAST_STEPS == k and rows % Q_TILE == 0
    return pl.BlockSpec((None,) * len(which) + (rows, n),
                        lambda s: (*which, jnp.minimum(s, CAST_STEPS - 1), 0))


def _cast_chunk(chunk_ref, w_ref, step):
    rows = chunk_ref.shape[0]
    w_ref[pl.ds(pl.multiple_of(step * rows, rows), rows), :] = chunk_ref[...].astype(bf16)


def _params(n_axes):
    return pltpu.CompilerParams(dimension_semantics=("arbitrary",) * n_axes,
                                vmem_limit_bytes=VMEM_LIMIT)


def _ffn_kernel(*refs, d_ff, fc, final, heads):
    if heads:
        (x_ref, of_ref, ob_ref, sg_ref, gn_ref, woc_ref, g_ref, w13c_ref, w2c_ref, fg_ref,
         o_ref, a_ref, w13_ref, w2_ref, y_ref, wo_ref) = refs
    else:
        x_ref, g_ref, w13c_ref, w2c_ref, fg_ref, o_ref, a_ref, w13_ref, w2_ref = refs
    step = pl.program_id(0)

    @pl.when(step < CAST_STEPS)
    def _():
        _cast_chunk(w13c_ref, w13_ref, step)
        _cast_chunk(w2c_ref, w2_ref, step)
        if heads:
            _cast_chunk(woc_ref, wo_ref, step)

    @pl.when(step >= CAST_STEPS)
    def _():
        if heads:
            for h in range(heads):
                hs = slice(h * HEAD_DIM, (h + 1) * HEAD_DIM)
                o = of_ref[:, hs] + ob_ref[:, hs]
                o = o * lax.rsqrt(jnp.mean(o * o, axis=-1, keepdims=True) + EPS)
                y_ref[:, hs] = (o * gn_ref[:, hs] * sg_ref[:, hs]).astype(bf16)
            x = x_ref[...] + jnp.dot(y_ref[...], wo_ref[...], preferred_element_type=f32)
        else:
            x = x_ref[...]
        h = _rms(x, g_ref[...]).astype(bf16)
        for j in range(d_ff // fc):
            gate = jnp.dot(h, w13_ref[:, j * fc:(j + 1) * fc], preferred_element_type=f32)
            up = jnp.dot(h, w13_ref[:, d_ff + j * fc:d_ff + (j + 1) * fc],
                         preferred_element_type=f32)
            a_ref[:, j * fc:(j + 1) * fc] = (jax.nn.silu(gate) * up).astype(bf16)
        y = jnp.dot(a_ref[...], w2_ref[...], preferred_element_type=f32)
        out = x + FFN_RES * y
        if final:
            out = _rms(out, fg_ref[...])
        o_ref[...] = out


def _ffn(x, g, w13_stack, w2_stack, which, final_g, *, final, hgrn_out=None):
    t, d = x.shape
    d_ff = w2_stack.shape[-2]
    tm = min(TOKEN_TILE, t)
    fc = MXU_COLS if d_ff % MXU_COLS == 0 else d_ff
    row = pl.BlockSpec((tm, d), lambda s: (jnp.maximum(s - CAST_STEPS, 0), 0))
    args = [x]
    specs = [row]
    scratch = [pltpu.VMEM((tm, d_ff), bf16), pltpu.VMEM(w13_stack.shape[-2:], bf16),
               pltpu.VMEM(w2_stack.shape[-2:], bf16)]
    if hgrn_out is not None:
        o_f, o_b, sg, gn, wo_stack, wo_which = hgrn_out
        args += [o_f, o_b, sg, gn.reshape(1, d), wo_stack]
        specs += [row, row, row, _resident((1, d)), _weight_chunks(wo_stack, wo_which)]
        scratch += [pltpu.VMEM((tm, d), bf16), pltpu.VMEM(wo_stack.shape[-2:], bf16)]
    args += [g.reshape(1, d), w13_stack, w2_stack, final_g.reshape(1, d)]
    specs += [_resident((1, d)), _weight_chunks(w13_stack, which),
              _weight_chunks(w2_stack, which), _resident((1, d))]
    return pl.pallas_call(
        functools.partial(_ffn_kernel, d_ff=d_ff, fc=fc, final=final,
                          heads=0 if hgrn_out is None else d // HEAD_DIM),
        out_shape=jax.ShapeDtypeStruct((t, d), f32),
        grid=(CAST_STEPS + t // tm,),
        in_specs=specs,
        out_specs=row,
        scratch_shapes=scratch,
        compiler_params=_params(1),
        name="ffn",
    )(*args)


def _interleave_matrix(tm):
    p = tm // SUBLANES
    rho = np.arange(tm)
    to_il = np.zeros((tm, tm), np.float32)
    to_il[rho, (rho % SUBLANES) * p + rho // SUBLANES] = 1.0
    return jnp.asarray(to_il, bf16), jnp.asarray(to_il.T, bf16)


def _conv_kernel(x_ref, xp_ref, xn_ref, g_ref, pm_ref, qm_ref, w1_ref, b1_ref, wdw_ref, bdw_ref,
                 lng_ref, lnb_ref, w2_ref, b2_ref, o_ref, hp_ref, u_ref, c_ref, *, tm, d):
    i = pl.program_id(1)
    n = pl.num_programs(1)
    p = tm // SUBLANES
    pad = CONV_PAD * SUBLANES
    h = _rms(x_ref[...], g_ref[...]).astype(bf16)
    hp_ref[:tm, :] = jnp.dot(pm_ref[...], h, preferred_element_type=f32).astype(bf16)
    hp_ref[tm:tm + HALO, :] = _rms(xp_ref[...], g_ref[...]).astype(bf16)
    hp_ref[tm + HALO:, :] = _rms(xn_ref[...], g_ref[...]).astype(bf16)

    row = lax.broadcasted_iota(jnp.int32, (tm + 2 * HALO, 1), 0)
    inside = (row < tm) | ((row < tm + HALO) & (i > 0)) | ((row >= tm + HALO) & (i < n - 1))
    sub = lax.broadcasted_iota(jnp.int32, (1, SUBLANES, 1), 1)

    for cb in range(d // MXU_COLS):
        cs = slice(cb * MXU_COLS, (cb + 1) * MXU_COLS)
        gs = slice(d + cb * MXU_COLS, d + (cb + 1) * MXU_COLS)
        a = jnp.dot(hp_ref[...], w1_ref[:, cs], preferred_element_type=f32) + b1_ref[:, cs]
        gate = jnp.dot(hp_ref[...], w1_ref[:, gs], preferred_element_type=f32) + b1_ref[:, gs]
        u = jnp.where(inside, a * jax.nn.sigmoid(gate), 0.0)
        u_ref[pad:pad + tm, cs] = u[:tm]
        tail = pltpu.roll(u[tm - pad:tm].reshape(CONV_PAD, SUBLANES, MXU_COLS), 1, axis=1)
        head = pltpu.roll(u[:pad].reshape(CONV_PAD, SUBLANES, MXU_COLS), SUBLANES - 1, axis=1)
        before = u[tm + HALO - CONV_PAD:tm + HALO][:, None, :]
        after = u[tm + HALO:tm + HALO + CONV_PAD][:, None, :]
        u_ref[:pad, cs] = jnp.where(sub == 0, before, tail).reshape(pad, MXU_COLS)
        u_ref[pad + tm:, cs] = jnp.where(sub == SUBLANES - 1, after, head).reshape(pad, MXU_COLS)

    def group(gi, carry):
        base = pl.multiple_of(gi * (CONV_GROUP * SUBLANES), CONV_GROUP * SUBLANES)
        for c in range(d // CONV_LANES):
            cs = slice(c * CONV_LANES, (c + 1) * CONV_LANES)
            taps = {}
            acc = [None] * CONV_GROUP
            for m in range(CONV_GROUP + CONV_WIDTH - 1):
                rows = u_ref[pl.ds(base + m * SUBLANES, SUBLANES), cs]
                for q in range(CONV_GROUP):
                    k = m - q
                    if 0 <= k < CONV_WIDTH:
                        if k not in taps:
                            taps[k] = wdw_ref[k * SUBLANES:(k + 1) * SUBLANES, cs]
                        term = rows * taps[k]
                        acc[q] = term if acc[q] is None else acc[q] + term
            for q in range(CONV_GROUP):
                c_ref[pl.ds(base + q * SUBLANES, SUBLANES), cs] = acc[q] + bdw_ref[:, cs]
        return carry

    lax.fori_loop(0, p // CONV_GROUP, group, 0)

    cv = c_ref[...]
    mu = jnp.mean(cv, axis=-1, keepdims=True)
    cc = cv - mu
    var = jnp.mean(cc * cc, axis=-1, keepdims=True)
    z = cc * lax.rsqrt(var + EPS) * lng_ref[...] + lnb_ref[...]
    z = jax.nn.silu(z).astype(bf16)
    z = jnp.dot(qm_ref[...], z, preferred_element_type=f32).astype(bf16)
    mix = jnp.dot(z, w2_ref[...], preferred_element_type=f32) + b2_ref[...]
    o_ref[...] = x_ref[...] + mix


def _conv_mixer(x, g, w1, b1, wdw, bdw, lng, lnb, w2, b2):
    b, s, d = x.shape
    tm = min(TOKEN_TILE, s)
    nh = tm // HALO
    last = s // HALO - 1
    p = tm // SUBLANES
    assert tm % SUBLANES == 0 and p % CONV_GROUP == 0 and p >= CONV_PAD and HALO >= CONV_PAD
    pm, qm = _interleave_matrix(tm)
    wdw8 = jnp.repeat(wdw, SUBLANES, axis=0)
    main = pl.BlockSpec((None, tm, d), lambda bi, i: (bi, i, 0))
    prev = pl.BlockSpec((None, HALO, d), lambda bi, i: (bi, jnp.maximum(i * nh - 1, 0), 0))
    nxt = pl.BlockSpec((None, HALO, d), lambda bi, i: (bi, jnp.minimum((i + 1) * nh, last), 0))
    vec = lambda a: a.reshape(1, -1)
    return pl.pallas_call(
        functools.partial(_conv_kernel, tm=tm, d=d),
        out_shape=jax.ShapeDtypeStruct((b, s, d), f32),
        grid=(b, s // tm),
        in_specs=[main, prev, nxt, _resident((1, d)), _resident(pm.shape), _resident(qm.shape),
                  _resident(w1.shape), _resident((1, 2 * d)), _resident(wdw8.shape),
                  _resident((1, d)), _resident((1, d)), _resident((1, d)), _resident(w2.shape),
                  _resident((1, d))],
        out_specs=main,
        scratch_shapes=[pltpu.VMEM((tm + 2 * HALO, d), bf16),
                        pltpu.VMEM((tm + 2 * CONV_PAD * SUBLANES, d), f32),
                        pltpu.VMEM((tm, d), f32)],
        compiler_params=_params(2),
        name="conv_mixer",
    )(x, x, x, vec(g), pm, qm, w1, vec(b1), wdw8, vec(bdw), vec(lng), vec(lnb), w2, vec(b2))


def _hproj_kernel(x_ref, g_ref, wc_ref, lbf_ref, lbb_ref,
                  q_ref, v_ref, lff_ref, lfb_ref, sg_ref, w_ref, *, d):
    step = pl.program_id(0)

    @pl.when(step < CAST_STEPS)
    def _():
        _cast_chunk(wc_ref, w_ref, step)

    @pl.when(step >= CAST_STEPS)
    def _():
        h = _rms(x_ref[...], g_ref[...]).astype(bf16)

        for c in range(d // MXU_COLS):
            cs = slice(c * MXU_COLS, (c + 1) * MXU_COLS)

            def proj(j):
                return jnp.dot(h, w_ref[:, j * d + c * MXU_COLS:j * d + (c + 1) * MXU_COLS],
                               preferred_element_type=f32)

            q_ref[:, cs] = jax.nn.silu(proj(0)).astype(bf16)
            v_ref[:, cs] = proj(1).astype(bf16)
            for j, lb_ref, lf_ref in ((2, lbf_ref, lff_ref), (3, lbb_ref, lfb_ref)):
                lb = lb_ref[:, cs]
                lf_ref[:, cs] = jnp.log(lb + (1.0 - lb) * jax.nn.sigmoid(proj(j)))
            sg_ref[:, cs] = jax.nn.silu(proj(4))


def _hgrn_proj(x, g, w_stack, which, lb_f, lb_b):
    t, d = x.shape
    tm = min(PROJ_TILE, t)
    row = pl.BlockSpec((tm, d), lambda s: (jnp.maximum(s - CAST_STEPS, 0), 0))
    sd = lambda dt: jax.ShapeDtypeStruct((t, d), dt)
    return pl.pallas_call(
        functools.partial(_hproj_kernel, d=d),
        out_shape=(sd(bf16), sd(bf16), sd(f32), sd(f32), sd(f32)),
        grid=(CAST_STEPS + t // tm,),
        in_specs=[row, _resident((1, d)), _weight_chunks(w_stack, which), _resident((1, d)),
                  _resident((1, d))],
        out_specs=(row,) * 5,
        scratch_shapes=[pltpu.VMEM(w_stack.shape[-2:], bf16)],
        compiler_params=_params(1),
        name="hgrn_proj",
    )(x, g.reshape(1, d), w_stack, lb_f.reshape(1, d), lb_b.reshape(1, d))


def _chunk_cumsum(lf_ref, b_ref, k_ref, chunk, reverse):
    n, d = lf_ref.shape
    sub = lax.broadcasted_iota(jnp.int32, (SUBLANES, 1), 0)
    peak = jnp.zeros((SUBLANES, SCAN_LANES), f32)
    for c in range(n // chunk):
        for l in range(d // SCAN_LANES):
            ls = slice(l * SCAN_LANES, (l + 1) * SCAN_LANES)
            tiles = range(chunk // SUBLANES)
            carry = None
            for g in (reversed(tiles) if reverse else tiles):
                rs = slice(c * chunk + g * SUBLANES, c * chunk + (g + 1) * SUBLANES)
                t = lf_ref[rs, ls]
                k_ref[rs, ls] = 1.0 - jnp.exp(t)
                shift = 1
                while shift < SUBLANES:
                    if reverse:
                        moved = pltpu.roll(t, SUBLANES - shift, axis=0)
                        keep = sub < SUBLANES - shift
                    else:
                        moved = pltpu.roll(t, shift, axis=0)
                        keep = sub >= shift
                    t = t + jnp.where(keep, moved, 0.0)
                    shift *= 2
                if carry is not None:
                    t = t + carry
                b_ref[rs, ls] = t
                peak = jnp.maximum(peak, jnp.abs(t))
                end = t[:1] if reverse else t[SUBLANES - 1:]
                carry = jnp.broadcast_to(end, (SUBLANES, SCAN_LANES))
    return jnp.max(peak, keepdims=True)


def _scan_factorised(q_ref, v_ref, k_ref, b_ref, o_ref, st_ref, *, reverse, chunk, heads):
    n = q_ref.shape[0]
    ti = lax.broadcasted_iota(jnp.int32, (chunk, chunk), 0)
    si = lax.broadcasted_iota(jnp.int32, (chunk, chunk), 1)
    causal = (si >= ti) if reverse else (si <= ti)
    order = range(n // chunk - 1, -1, -1) if reverse else range(n // chunk)
    for c in order:
        rs = slice(c * chunk, (c + 1) * chunk)
        b = b_ref[rs, :]
        b_end = b[:1] if reverse else b[chunk - 1:]
        mid = 0.5 * b_end
        e_mid = jnp.exp(mid)
        e_end = jnp.exp(b_end)
        e_up = jnp.exp(b - mid)
        qd = q_ref[rs, :] * e_up
        kd = k_ref[rs, :] / e_up
        qs = (qd * e_mid).astype(bf16)
        ks = (kd * e_mid).astype(bf16)
        qd = qd.astype(bf16)
        kd = kd.astype(bf16)
        for h in range(heads):
            hs = slice(h * HEAD_DIM, (h + 1) * HEAD_DIM)
            v = v_ref[rs, hs]
            sc = lax.dot_general(qd[:, hs], kd[:, hs], (((1,), (1,)), ((), ())),
                                 preferred_element_type=f32)
            sc = jnp.where(causal, sc, 0.0).astype(bf16)
            st = st_ref[h]
            o_ref[rs, hs] = (jnp.dot(sc, v, preferred_element_type=f32)
                             + lax.dot_general(qs[:, hs], st.astype(bf16),
                                               (((1,), (1,)), ((), ())),
                                               preferred_element_type=f32))
            st_ref[h] = st * e_end[:, hs] + lax.dot_general(
                v, ks[:, hs], (((0,), (0,)), ((), ())), preferred_element_type=f32)


def _scan_direct(q_ref, v_ref, k_ref, b_ref, o_ref, st_ref, *, reverse, chunk, heads):
    n = q_ref.shape[0]
    nc = n // chunk
    pos = lax.broadcasted_iota(jnp.int32, (chunk, 1), 0)
    tile_pos = lax.broadcasted_iota(jnp.int32, (Q_TILE, 1), 0)

    def chunk_body(ci, carry):
        c = nc - 1 - ci if reverse else ci
        r0 = pl.multiple_of(c * chunk, chunk)
        rs = pl.ds(r0, chunk)

        def row_body(t, carry):
            b = b_ref[rs, :]
            diff = b_ref[pl.ds(r0 + t, 1), :] - b
            seen = (pos >= t) if reverse else (pos <= t)
            t0 = pl.multiple_of((t // Q_TILE) * Q_TILE, Q_TILE)
            q_tile = q_ref[pl.ds(r0 + t0, Q_TILE), :].astype(f32)
            q_row = jnp.sum(jnp.where(tile_pos == t - t0, q_tile, 0.0), axis=0, keepdims=True)
            p = q_row * k_ref[rs, :] * jnp.exp(jnp.where(seen, diff, -jnp.inf))
            outs = []
            for h in range(heads):
                hs = slice(h * HEAD_DIM, (h + 1) * HEAD_DIM)
                sc = jnp.sum(p[:, hs], axis=-1, keepdims=True)
                outs.append(jnp.sum(sc * v_ref[rs, hs].astype(f32), axis=0, keepdims=True))
            o_ref[pl.ds(r0 + t, 1), :] = jnp.concatenate(outs, axis=1)
            return carry

        lax.fori_loop(0, chunk, row_body, 0)
        b = b_ref[rs, :]
        b_end = b[:1] if reverse else b[chunk - 1:]
        e_end = jnp.exp(b_end)
        qs = (q_ref[rs, :] * jnp.exp(b)).astype(bf16)
        ks = (k_ref[rs, :] * jnp.exp(b_end - b)).astype(bf16)
        for h in range(heads):
            hs = slice(h * HEAD_DIM, (h + 1) * HEAD_DIM)
            st = st_ref[h]
            o_ref[rs, hs] += lax.dot_general(qs[:, hs], st.astype(bf16),
                                             (((1,), (1,)), ((), ())),
                                             preferred_element_type=f32)
            st_ref[h] = st * e_end[:, hs] + lax.dot_general(
                v_ref[rs, hs], ks[:, hs], (((0,), (0,)), ((), ())),
                preferred_element_type=f32)
        return carry

    lax.fori_loop(0, nc, chunk_body, 0)


def _scan_kernel(qf_ref, vf_ref, lff_ref, lff0_ref, qb_ref, vb_ref, lfb_ref, lfb0_ref,
                 of_ref, ob_ref, stf_ref, stb_ref, bf_ref, bb_ref, kf_ref, kb_ref, tame_ref,
                 *, chunk, heads):
    i = pl.program_id(1)
    slot = i % 2
    nxt = 1 - slot

    def prepare(lf_f, lf_b, dst):
        decay = jnp.maximum(
            _chunk_cumsum(lf_f, bf_ref.at[dst], kf_ref.at[dst], chunk, reverse=False),
            _chunk_cumsum(lf_b, bb_ref.at[dst], kb_ref.at[dst], chunk, reverse=True))[0, 0]
        tame_ref[dst] = (decay <= MAX_CHUNK_DECAY).astype(jnp.int32)

    @pl.when(i == 0)
    def _():
        stf_ref[...] = jnp.zeros_like(stf_ref)
        stb_ref[...] = jnp.zeros_like(stb_ref)
        prepare(lff0_ref, lfb0_ref, slot)

    fwd = (qf_ref, vf_ref, kf_ref.at[slot], bf_ref.at[slot], of_ref, stf_ref)
    bwd = (qb_ref, vb_ref, kb_ref.at[slot], bb_ref.at[slot], ob_ref, stb_ref)
    tame = tame_ref[slot] == 1

    @pl.when(tame)
    def _():
        _scan_factorised(*fwd, reverse=False, chunk=chunk, heads=heads)
        _scan_factorised(*bwd, reverse=True, chunk=chunk, heads=heads)
        prepare(lff_ref, lfb_ref, nxt)

    @pl.when(jnp.logical_not(tame))
    def _():
        _scan_direct(*fwd, reverse=False, chunk=chunk, heads=heads)
        _scan_direct(*bwd, reverse=True, chunk=chunk, heads=heads)
        prepare(lff_ref, lfb_ref, nxt)


def _hgrn_scan(q, v, lff, lfb):
    b, s, d = q.shape
    blk = min(SCAN_BLOCK, s)
    chunk = min(SCAN_CHUNK, blk)
    nb = s // blk
    heads = d // HEAD_DIM
    fwd = pl.BlockSpec((None, blk, d), lambda bi, i: (bi, i, 0))
    bwd = pl.BlockSpec((None, blk, d), lambda bi, i: (bi, nb - 1 - i, 0))
    fwd_next = pl.BlockSpec((None, blk, d), lambda bi, i: (bi, jnp.minimum(i + 1, nb - 1), 0))
    bwd_next = pl.BlockSpec((None, blk, d), lambda bi, i: (bi, jnp.maximum(nb - 2 - i, 0), 0))
    fwd_first = pl.BlockSpec((None, blk, d), lambda bi, i: (bi, 0, 0))
    bwd_first = pl.BlockSpec((None, blk, d), lambda bi, i: (bi, nb - 1, 0))
    out = jax.ShapeDtypeStruct((b, s, d), f32)
    state = pltpu.VMEM((heads, HEAD_DIM, HEAD_DIM), f32)
    slots = pltpu.VMEM((2, blk, d), f32)
    return pl.pallas_call(
        functools.partial(_scan_kernel, chunk=chunk, heads=heads),
        out_shape=(out, out),
        grid=(b, nb),
        in_specs=[fwd, fwd, fwd_next, fwd_first, bwd, bwd, bwd_next, bwd_first],
        out_specs=(fwd, bwd),
        scratch_shapes=[state, state, slots, slots, slots, slots, pltpu.SMEM((2,), jnp.int32)],
        compiler_params=_params(2),
        name="hgrn_scan",
    )(q, v, lff, lff, q, v, lfb, lfb)


def kernel(x, norm_g, ffn_w13, ffn_w2, conv_w_pw1, conv_b_pw1, conv_w_dw, conv_b_dw,
           conv_ln_g, conv_ln_b, conv_w_pw2, conv_b_pw2, hgrn_w_in, hgrn_lb,
           hgrn_gn_g, hgrn_w_out, final_g):
    b, s, d = x.shape
    depth = norm_g.shape[0]
    t = b * s
    lb_table = jnp.cumsum(jax.nn.softmax(hgrn_lb.astype(f32), axis=1), axis=1)
    lb_table = lb_table - lb_table[:, :1]
    wb = lambda w: w.astype(bf16)

    x = x.reshape(t, d)
    for layer in range(depth):
        x = _ffn(x, norm_g[layer, 0], ffn_w13, ffn_w2, (layer, 0), final_g, final=False)
        j = layer // 2
        hgrn_out = None
        if layer % 2 == 0:
            x = _conv_mixer(x.reshape(b, s, d), norm_g[layer, 1], wb(conv_w_pw1[j]),
                            conv_b_pw1[j], conv_w_dw[j], conv_b_dw[j], conv_ln_g[j],
                            conv_ln_b[j], wb(conv_w_pw2[j]), conv_b_pw2[j]).reshape(t, d)
        else:
            q, v, lff, lfb, sg = _hgrn_proj(
                x, norm_g[layer, 1], hgrn_w_in, (j,), lb_table[0, layer], lb_table[1, layer])
            r3 = lambda a: a.reshape(b, s, d)
            o_f, o_b = _hgrn_scan(r3(q), r3(v), r3(lff), r3(lfb))
            hgrn_out = (o_f.reshape(t, d), o_b.reshape(t, d), sg, hgrn_gn_g[j], hgrn_w_out, (j,))
        x = _ffn(x, norm_g[layer, 2], ffn_w13, ffn_w2, (layer, 1), final_g,
                 final=(layer == depth - 1), hgrn_out=hgrn_out)
    return x.reshape(b, s, d)
```
